```python
import jax, jax.numpy as jnp
from jax import lax
import numpy as np

D_MODEL = 4096
BATCH = 4
SEQ = 2048
DEPTH = 4
DEC_BATCH = 128
DEC_SEQ = 4
PAST_LEN = 16384
PAGE_SIZE = 128

D_MIX = D_MODEL
HEAD_DIM = 128
HG_WIDTH = D_MIX // 2
GDN_WIDTH = D_MIX - HG_WIDTH
HG_HEADS = HG_WIDTH // HEAD_DIM
HG_DK = 128
HG_DV = HG_WIDTH // HG_HEADS
GDN_HEADS = GDN_WIDTH // HEAD_DIM
GDN_DK = HEAD_DIM
GDN_DV = GDN_WIDTH // GDN_HEADS
CONV_W = 4
GDN_CONV_DIM = GDN_HEADS * (2 * GDN_DK + GDN_DV)
D_FF = ((8 * D_MODEL // 3 + 255) // 256) * 256
CHUNK = 64
EPS = 1e-6
F_FLOOR = 1e-20
IN_SIZES = (HG_HEADS * HG_DK, HG_HEADS * HG_DK, HG_WIDTH, HG_WIDTH,
            GDN_CONV_DIM, GDN_HEADS, GDN_HEADS, GDN_WIDTH)
D_IN = sum(IN_SIZES)

kernel_name = "hymba_hgrn2_gdn_macaron_step"


def _rmsnorm(x, w):
    xf = x.astype(jnp.float32)
    y = xf * lax.rsqrt(jnp.mean(xf * xf, axis=-1, keepdims=True) + EPS)
    return (y * w.astype(jnp.float32)).astype(x.dtype)


def _head_rmsnorm(o, w):
    return o * lax.rsqrt(jnp.mean(o * o, axis=-1, keepdims=True) + EPS) * w.astype(jnp.float32)


def _l2norm(x):
    xf = x.astype(jnp.float32)
    return xf * lax.rsqrt(jnp.sum(xf * xf, axis=-1, keepdims=True) + EPS)


def _swiglu(h, w_gu, w_down):
    gate, up = jnp.split(h @ w_gu, 2, axis=-1)
    return (jax.nn.silu(gate) * up) @ w_down


def _chunk(t, C):
    B, T = t.shape[:2]
    n = -(-T // C)
    t = jnp.pad(t.astype(jnp.float32), [(0, 0), (0, n * C - T)] + [(0, 0)] * (t.ndim - 2))
    t = t.reshape((B, n, C) + t.shape[2:])
    perm = (1, 0, 3, 2) + tuple(range(4, t.ndim))
    return t.transpose(perm)


def _unchunk(o, T):
    n, B, H, C, D = o.shape
    return o.transpose(1, 0, 3, 2, 4).reshape(B, n * C, H, D)[:, :T]


def _hgrn2_recurrence(q, k, v, logf, S0):
    T = q.shape[1]
    C = min(CHUNK, T)
    causal = jnp.tril(jnp.ones((C, C), dtype=bool))[:, :, None]
    qc, kc, vc, gc = (_chunk(t, C) for t in (q, k, v, logf))

    def step(S, inp):
        qi, ki, vi, gi = inp
        b = jnp.cumsum(gi, axis=2)
        diff = b[:, :, :, None, :] - b[:, :, None, :, :]
        decay = jnp.where(causal, jnp.exp(jnp.where(causal, diff, 0.0)), 0.0)
        attn = jnp.einsum('bhtd,bhsd,bhtsd->bhts', qi, ki, decay)
        o = (jnp.einsum('bhts,bhsv->bhtv', attn, vi)
             + jnp.einsum('bhtd,bhdv->bhtv', qi * jnp.exp(b), S))
        b_last = b[:, :, -1:, :]
        S = (jnp.exp(b_last[:, :, 0, :])[..., None] * S
             + jnp.einsum('bhsd,bhsv->bhdv', ki * jnp.exp(b_last - b), vi))
        return S, o

    S, o = lax.scan(step, S0.astype(jnp.float32), (qc, kc, vc, gc))
    return _unchunk(o, T), S


def _gated_delta_recurrence(q, k, v, g, beta, S0):
    T = q.shape[1]
    C = min(CHUNK, T)
    DV = v.shape[-1]
    causal = jnp.tril(jnp.ones((C, C), dtype=bool))
    strict = jnp.tril(jnp.ones((C, C), dtype=bool), -1)
    eye = jnp.eye(C, dtype=jnp.float32)
    qc, kc, vc, gc, bc = (_chunk(t, C) for t in (q, k, v, g, beta))

    def step(S, inp):
        qi, ki, vi, gi, bi = inp
        d = jnp.cumsum(gi, axis=-1)
        dd = d[..., :, None] - d[..., None, :]
        L = jnp.where(causal, jnp.exp(jnp.where(causal, dd, 0.0)), 0.0)
        kk = jnp.einsum('bhtd,bhsd->bhts', ki, ki)
        M = jnp.where(strict, bi[..., :, None] * kk * L, 0.0)
        rhs = jnp.concatenate([vi * bi[..., None], ki * (bi * jnp.exp(d))[..., None]], axis=-1)
        X = lax.linalg.triangular_solve(eye + M, rhs, left_side=True, lower=True,
                                        unit_diagonal=True)
        u, w = X[..., :DV], X[..., DV:]
        v_new = u - jnp.einsum('bhtd,bhdv->bhtv', w, S)
        qk = jnp.einsum('bhtd,bhsd->bhts', qi, ki) * L
        o = (jnp.einsum('bhtd,bhdv->bhtv', qi * jnp.exp(d)[..., None], S)
             + jnp.einsum('bhts,bhsv->bhtv', qk, v_new))
        d_last = d[..., -1:]
        S = (jnp.exp(d_last)[..., None] * S
             + jnp.einsum('bhsd,bhsv->bhdv', ki * jnp.exp(d_last - d)[..., None], v_new))
        return S, o

    S, o = lax.scan(step, S0.astype(jnp.float32), (qc, kc, vc, gc, bc))
    return _unchunk(o, T), S


def _causal_conv(x, buf, w):
    T = x.shape[1]
    xx = jnp.concatenate([buf.astype(x.dtype), x], axis=1)
    y = xx[:, 0:T] * w[0]
    for j in range(1, CONV_W):
        y = y + xx[:, j:j + T] * w[j]
    return y, xx[:, xx.shape[1] - (CONV_W - 1):]


def _mixer(h, w_in, lb, hg_norm, conv_w, A_log, dt_bias, gdn_norm, w_out, S_hg, S_gdn, buf):
    B, T, _ = h.shape
    offs = np.cumsum(IN_SIZES)[:-1].tolist()
    hq, hf, hi, hgate, qkv, a, b, z = jnp.split(h @ w_in, offs, axis=-1)

    q = jax.nn.silu(hq.astype(jnp.float32)).reshape(B, T, HG_HEADS, HG_DK)
    zf = hf.astype(jnp.float32).reshape(B, T, HG_HEADS, HG_DK)
    lbf = lb.astype(jnp.float32).reshape(HG_HEADS, HG_DK)
    f = lbf + (1.0 - lbf) * jax.nn.sigmoid(zf)
    logf = jnp.log(jnp.maximum(f, F_FLOOR))
    k = (1.0 - lbf) * jax.nn.sigmoid(-zf)
    v = hi.reshape(B, T, HG_HEADS, HG_DV)
    o_hg, S_hg_new = _hgrn2_recurrence(q, k, v, logf, S_hg)
    o_hg = _head_rmsnorm(o_hg, hg_norm) * jax.nn.sigmoid(
        hgate.astype(jnp.float32).reshape(B, T, HG_HEADS, HG_DV))

    qkv_c, buf_new = _causal_conv(qkv, buf, conv_w)
    qkv_c = jax.nn.silu(qkv_c)
    nq = GDN_HEADS * GDN_DK
    gq = _l2norm(qkv_c[..., :nq].reshape(B, T, GDN_HEADS, GDN_DK)) * (GDN_DK ** -0.5)
    gk = _l2norm(qkv_c[..., nq:2 * nq].reshape(B, T, GDN_HEADS, GDN_DK))
    gv = qkv_c[..., 2 * nq:].reshape(B, T, GDN_HEADS, GDN_DV)
    g = -jnp.exp(A_log.astype(jnp.float32)) * jax.nn.softplus(
        a.astype(jnp.float32) + dt_bias.astype(jnp.float32))
    beta = jax.nn.sigmoid(b.astype(jnp.float32))
    o_gdn, S_gdn_new = _gated_delta_recurrence(gq, gk, gv, g, beta, S_gdn)
    o_gdn = _head_rmsnorm(o_gdn, gdn_norm) * jax.nn.silu(
        z.astype(jnp.float32).reshape(B, T, GDN_HEADS, GDN_DV))

    o = jnp.concatenate([o_hg.reshape(B, T, HG_WIDTH), o_gdn.reshape(B, T, GDN_WIDTH)], axis=-1)
    return o.astype(h.dtype) @ w_out, S_hg_new, S_gdn_new, buf_new


def _trunk(x, s_hg, s_gdn, s_conv, lb_table, ffn1_norm, ffn1_w_gu, ffn1_w_down, mix_norm, w_in,
           hg_norm, gdn_conv_w, gdn_A_log, gdn_dt_bias, gdn_norm, w_out,
           ffn2_norm, ffn2_w_gu, ffn2_w_down, final_norm):
    hg_out, gdn_out, conv_out = [], [], []
    for l in range(DEPTH):
        x = x + 0.5 * _swiglu(_rmsnorm(x, ffn1_norm[l]), ffn1_w_gu[l], ffn1_w_down[l])
        m, S_hg, S_gdn, buf = _mixer(_rmsnorm(x, mix_norm[l]), w_in[l], lb_table[l], hg_norm[l],
                                     gdn_conv_w[l], gdn_A_log[l], gdn_dt_bias[l], gdn_norm[l],
                                     w_out[l], s_hg[l], s_gdn[l], s_conv[l])
        x = x + m
        x = x + 0.5 * _swiglu(_rmsnorm(x, ffn2_norm[l]), ffn2_w_gu[l], ffn2_w_down[l])
        hg_out.append(S_hg)
        gdn_out.append(S_gdn)
        conv_out.append(buf)
    return _rmsnorm(x, final_norm), jnp.stack(hg_out), jnp.stack(gdn_out), jnp.stack(conv_out)


def setup_inputs(seed: int = 0) -> dict:
    key = jax.random.key(seed)
    ks = jax.random.split(key, 24)
    f32 = jnp.float32
    nrm = lambda k, shape, s: jax.random.normal(k, shape, f32) * s
    dt = jnp.exp(jax.random.uniform(ks[13], (DEPTH, GDN_HEADS), f32, np.log(1e-3), np.log(1e-1)))
    return {
        "x_prompt": nrm(ks[0], (BATCH, SEQ, D_MODEL), 1.0),
        "x_sample": nrm(ks[1], (DEC_BATCH, DEC_SEQ, D_MODEL), 1.0),
        "state_hgrn": nrm(ks[2], (DEPTH, DEC_BATCH, HG_HEADS, HG_DK, HG_DV), 0.5),
        "state_gdn": nrm(ks[3], (DEPTH, DEC_BATCH, GDN_HEADS, GDN_DK, GDN_DV), 0.5),
        "state_conv": nrm(ks[4], (DEPTH, DEC_BATCH, CONV_W - 1, GDN_CONV_DIM), 1.0),
        "ffn1_norm": 1.0 + nrm(ks[5], (DEPTH, D_MODEL), 0.02),
        "ffn1_w_gu": nrm(ks[6], (DEPTH, D_MODEL, 2 * D_FF), D_MODEL ** -0.5),
        "ffn1_w_down": nrm(ks[7], (DEPTH, D_FF, D_MODEL), D_FF ** -0.5),
        "mix_norm": 1.0 + nrm(ks[8], (DEPTH, D_MODEL), 0.02),
        "w_in": nrm(ks[9], (DEPTH, D_MODEL, D_IN), D_MODEL ** -0.5),
        "hg_lower_bounds": 1.0 + nrm(ks[10], (DEPTH, HG_HEADS * HG_DK), 0.1),
        "hg_norm": 1.0 + nrm(ks[11], (DEPTH, HG_DV), 0.02),
        "gdn_conv_w": nrm(ks[12], (DEPTH, CONV_W, GDN_CONV_DIM), CONV_W ** -0.5),
        "gdn_A_log": jnp.log(jax.random.uniform(ks[14], (DEPTH, GDN_HEADS), f32, 1.0, 16.0)),
        "gdn_dt_bias": jnp.log(jnp.expm1(dt)),
        "gdn_norm": 1.0 + nrm(ks[15], (DEPTH, GDN_DV), 0.02),
        "w_out": nrm(ks[16], (DEPTH, D_MIX, D_MODEL), D_MIX ** -0.5),
        "ffn2_norm": 1.0 + nrm(ks[17], (DEPTH, D_MODEL), 0.02),
        "ffn2_w_gu": nrm(ks[18], (DEPTH, D_MODEL, 2 * D_FF), D_MODEL ** -0.5),
        "ffn2_w_down": nrm(ks[19], (DEPTH, D_FF, D_MODEL), D_FF ** -0.5),
        "final_norm": 1.0 + nrm(ks[20], (D_MODEL,), 0.02),
    }


def reference(x_prompt, x_sample, state_hgrn, state_gdn, state_conv, ffn1_norm, ffn1_w_gu,
              ffn1_w_down, mix_norm, w_in, hg_lower_bounds, hg_norm, gdn_conv_w, gdn_A_log,
              gdn_dt_bias, gdn_norm, w_out, ffn2_norm, ffn2_w_gu, ffn2_w_down, final_norm):
    p = jax.nn.softmax(hg_lower_bounds.astype(jnp.float32), axis=0)
    lb_table = jnp.clip(jnp.cumsum(p, axis=0) - p[0:1], 0.0, 1.0)
    weights = (lb_table, ffn1_norm, ffn1_w_gu, ffn1_w_down, mix_norm, w_in, hg_norm, gdn_conv_w,
               gdn_A_log, gdn_dt_bias, gdn_norm, w_out, ffn2_norm, ffn2_w_gu, ffn2_w_down, final_norm)

    B = x_prompt.shape[0]
    zero_hg = jnp.zeros((DEPTH, B, HG_HEADS, HG_DK, HG_DV), jnp.float32)
    zero_gdn = jnp.zeros((DEPTH, B, GDN_HEADS, GDN_DK, GDN_DV), jnp.float32)
    zero_conv = jnp.zeros((DEPTH, B, CONV_W - 1, GDN_CONV_DIM), x_prompt.dtype)
    y_prompt, hgrn_prompt, gdn_prompt, conv_prompt = _trunk(
        x_prompt, zero_hg, zero_gdn, zero_conv, *weights)
    y_sample, hgrn_sample, gdn_sample, conv_sample = _trunk(
        x_sample, state_hgrn, state_gdn, state_conv, *weights)
    return (y_prompt, y_sample, hgrn_prompt, gdn_prompt, conv_prompt,
            hgrn_sample, gdn_sample, conv_sample)
```

```python
import functools

import jax
import jax.numpy as jnp
from jax import lax
from jax.experimental import pallas as pl
from jax.experimental.pallas import tpu as pltpu

D_MODEL = 4096
DEPTH = 4
HEAD_DIM = 128
HEADS = 16
WIDTH = HEADS * HEAD_DIM
CONV_W = 4
CONV_DIM = 3 * WIDTH
D_FF = 11008
EPS = 1e-6
F_FLOOR = 1e-20
LANES = 128
SUBLANES = 8
FF_TILE = 256
CHUNK = 64
SUB = 16
MAIN_COLS = 4 * WIDTH + CONV_DIM + WIDTH
VMEM_LIMIT = 56 * 1024 * 1024

BF16 = jnp.bfloat16
F32 = jnp.float32


def _params(sem, vmem=VMEM_LIMIT):
    return pltpu.CompilerParams(dimension_semantics=sem, vmem_limit_bytes=vmem)


def _bdot(a, b):
    return jnp.dot(a.astype(BF16), b.astype(BF16), preferred_element_type=F32)


def _bdot_nt(a, b):
    return lax.dot_general(a.astype(BF16), b.astype(BF16), (((1,), (1,)), ((), ())),
                           preferred_element_type=F32)


def _bdot_tn(a, b):
    return lax.dot_general(a.astype(BF16), b.astype(BF16), (((0,), (0,)), ((), ())),
                           preferred_element_type=F32)


def _sigmoid(x):
    return 1.0 / (1.0 + jnp.exp(-x))


def _silu(x):
    return x * _sigmoid(x)


def _softplus(x):
    return jnp.maximum(x, 0.0) + jnp.log(1.0 + jnp.exp(-jnp.abs(x)))


def _cumsum_rows(x):
    n = x.shape[0]
    row = lax.broadcasted_iota(jnp.int32, x.shape, 0)
    s = 1
    while s < n:
        x = x + jnp.where(row >= s, pltpu.roll(x, s, axis=0), 0.0)
        s *= 2
    return x


def _row_to_col(r):
    n = r.shape[1]
    ri = lax.broadcasted_iota(jnp.int32, (n, n), 0)
    ci = lax.broadcasted_iota(jnp.int32, (n, n), 1)
    return jnp.sum(jnp.where(ri == ci, jnp.broadcast_to(r, (n, n)), 0.0), axis=1, keepdims=True)


def _col_to_row(c):
    n = c.shape[0]
    ri = lax.broadcasted_iota(jnp.int32, (n, n), 0)
    ci = lax.broadcasted_iota(jnp.int32, (n, n), 1)
    return jnp.sum(jnp.where(ri == ci, jnp.broadcast_to(c, (n, n)), 0.0), axis=0, keepdims=True)


def _rmsnorm_body(x_ref, w_ref, o_ref):
    x = x_ref[...]
    y = x * lax.rsqrt(jnp.mean(x * x, axis=-1, keepdims=True) + EPS)
    o_ref[...] = (y * w_ref[...]).astype(o_ref.dtype)


def _rmsnorm(x, w, out_dtype, tm=256):
    m = x.shape[0]
    return pl.pallas_call(
        _rmsnorm_body,
        grid=(m // tm,),
        in_specs=[pl.BlockSpec((tm, D_MODEL), lambda i: (i, 0)),
                  pl.BlockSpec((1, D_MODEL), lambda i: (0, 0))],
        out_specs=pl.BlockSpec((tm, D_MODEL), lambda i: (i, 0)),
        out_shape=jax.ShapeDtypeStruct((m, D_MODEL), out_dtype),
        compiler_params=_params(("parallel",)),
        name="rmsnorm",
    )(x, w.reshape(1, D_MODEL))


def _mm_body(x_ref, w_ref, o_ref):
    o_ref[...] = jnp.dot(x_ref[...], w_ref[...], preferred_element_type=F32)


def _matmul(xn, w, layer, tm, tn):
    m, k = xn.shape
    n = w.shape[2]
    return pl.pallas_call(
        _mm_body,
        grid=(m // tm, n // tn),
        in_specs=[pl.BlockSpec((tm, k), lambda i, j: (i, 0)),
                  pl.BlockSpec((None, k, tn), lambda i, j: (layer, 0, j))],
        out_specs=pl.BlockSpec((tm, tn), lambda i, j: (i, j)),
        out_shape=jax.ShapeDtypeStruct((m, n), F32),
        compiler_params=_params(("parallel", "arbitrary")),
        name="matmul",
    )(xn, w)


def _swiglu_body(x_ref, wg_ref, wu_ref, o_ref):
    x = x_ref[...]
    g = jnp.dot(x, wg_ref[...], preferred_element_type=F32)
    u = jnp.dot(x, wu_ref[...], preferred_element_type=F32)
    o_ref[...] = (_silu(g) * u).astype(o_ref.dtype)


def _swiglu_up(xn, w_gu, layer, tm):
    m, k = xn.shape
    nj = D_FF // FF_TILE
    return pl.pallas_call(
        _swiglu_body,
        grid=(m // tm, nj),
        in_specs=[pl.BlockSpec((tm, k), lambda i, j: (i, 0)),
                  pl.BlockSpec((None, k, FF_TILE), lambda i, j: (layer, 0, j)),
                  pl.BlockSpec((None, k, FF_TILE), lambda i, j: (layer, 0, j + nj))],
        out_specs=pl.BlockSpec((tm, FF_TILE), lambda i, j: (i, j)),
        out_shape=jax.ShapeDtypeStruct((m, D_FF), BF16),
        compiler_params=_params(("parallel", "arbitrary")),
        name="swiglu_up",
    )(xn, w_gu, w_gu)


def _down_body(scale, a_ref, w_ref, r_ref, o_ref):
    acc = jnp.dot(a_ref[...], w_ref[...], preferred_element_type=F32)
    o_ref[...] = r_ref[...] + scale * acc


def _down_residual(a, w, layer, res, scale, tm, tn):
    m, k = a.shape
    return pl.pallas_call(
        functools.partial(_down_body, scale),
        grid=(m // tm, D_MODEL // tn),
        in_specs=[pl.BlockSpec((tm, k), lambda i, j: (i, 0), pipeline_mode=pl.Buffered(1)),
                  pl.BlockSpec((None, k, tn), lambda i, j: (layer, 0, j)),
                  pl.BlockSpec((tm, tn), lambda i, j: (i, j))],
        out_specs=pl.BlockSpec((tm, tn), lambda i, j: (i, j)),
        out_shape=jax.ShapeDtypeStruct((m, D_MODEL), F32),
        compiler_params=_params(("parallel", "arbitrary")),
        name="down_residual",
    )(a, w, res)


def _mixout_body(a1_ref, a2_ref, w1_ref, w2_ref, r_ref, o_ref):
    acc = jnp.dot(a1_ref[...].astype(BF16), w1_ref[...], preferred_element_type=F32)
    acc += jnp.dot(a2_ref[...].astype(BF16), w2_ref[...], preferred_element_type=F32)
    o_ref[...] = r_ref[...] + acc


def _mix_out_residual(o_hg, o_gdn, w_out, layer, res, tm, tn):
    m = o_hg.shape[0]
    return pl.pallas_call(
        _mixout_body,
        grid=(m // tm, D_MODEL // tn),
        in_specs=[pl.BlockSpec((tm, WIDTH), lambda i, j: (i, 0)),
                  pl.BlockSpec((tm, WIDTH), lambda i, j: (i, 0)),
                  pl.BlockSpec((None, WIDTH, tn), lambda i, j: (layer, 0, j)),
                  pl.BlockSpec((None, WIDTH, tn), lambda i, j: (layer, 1, j)),
                  pl.BlockSpec((tm, tn), lambda i, j: (i, j))],
        out_specs=pl.BlockSpec((tm, tn), lambda i, j: (i, j)),
        out_shape=jax.ShapeDtypeStruct((m, D_MODEL), F32),
        compiler_params=_params(("parallel", "arbitrary")),
        name="mix_out_residual",
    )(o_hg, o_gdn, w_out, w_out, res)


def _lb_body(p_ref, o_ref):
    x = p_ref[...]
    e = jnp.exp(x - jnp.max(x, axis=0, keepdims=True))
    p = e / jnp.sum(e, axis=0, keepdims=True)
    rows = [p[0:1]]
    for l in range(1, DEPTH):
        rows.append(rows[-1] + p[l:l + 1])
    cs = jnp.concatenate(rows, axis=0)
    o_ref[...] = jnp.clip(cs - p[0:1], 0.0, 1.0)


def _lower_bounds(hg_lower_bounds):
    return pl.pallas_call(
        _lb_body,
        out_shape=jax.ShapeDtypeStruct((DEPTH, WIDTH), F32),
        name="lower_bounds",
    )(hg_lower_bounds)


def _hgrn_gates(zq, zf, lb, valid=None):
    q = _silu(zq)
    f = lb + (1.0 - lb) * _sigmoid(zf)
    logf = jnp.log(jnp.maximum(f, F_FLOOR))
    k = (1.0 - lb) * _sigmoid(-zf)
    if valid is not None:
        logf = jnp.where(valid, logf, 0.0)
        k = jnp.where(valid, k, 0.0)
    return q, k, logf


def _hgrn_diag(qs, ks, bs, vs, n_valid):
    n = qs.shape[0]
    row = lax.broadcasted_iota(jnp.int32, (n, 1), 0)
    out = jnp.zeros_like(vs)
    for s in range(n_valid):
        mask = row >= s
        diff = jnp.where(mask, bs - bs[s:s + 1, :], 0.0)
        col = jnp.sum(qs * ks[s:s + 1, :] * jnp.exp(diff), axis=-1, keepdims=True)
        out = out + jnp.where(mask, col, 0.0) * vs[s:s + 1, :]
    return out


def _hgrn_chunk(q, k, v, logf, S, sub, n_valid):
    c = q.shape[0]
    b = _cumsum_rows(logf)
    o_inter = _bdot(q * jnp.exp(b), S)
    outs = []
    for i in range(c // sub):
        lo, hi = i * sub, (i + 1) * sub
        qs, ks, bs, vs = q[lo:hi], k[lo:hi], b[lo:hi], v[lo:hi]
        oi = o_inter[lo:hi] + _hgrn_diag(qs, ks, bs, vs, min(sub, max(n_valid - lo, 0)))
        if i > 0:
            ref = b[lo - 1:lo, :]
            att = _bdot_nt(qs * jnp.exp(bs - ref), k[:lo] * jnp.exp(ref - b[:lo]))
            oi = oi + _bdot(att, v[:lo])
        outs.append(oi)
    o = outs[0] if len(outs) == 1 else jnp.concatenate(outs, axis=0)
    bl = b[c - 1:c, :]
    upd = _bdot_tn(k * jnp.exp(bl - b), v)
    S_new = _row_to_col(jnp.exp(bl)) * S + upd
    return o, S_new


def _gdn_chunk(q, k, v, d_col, beta_col, S):
    c = q.shape[0]
    ri = lax.broadcasted_iota(jnp.int32, (c, c), 0)
    ci = lax.broadcasted_iota(jnp.int32, (c, c), 1)
    causal = ri >= ci
    strict = ri > ci
    d_row = _col_to_row(d_col)
    L = jnp.where(causal, jnp.exp(jnp.where(causal, d_col - d_row, 0.0)), 0.0)
    kk = _bdot_nt(k, k)
    qk = _bdot_nt(q, k) * L
    P = jnp.where(strict, -(beta_col * kk * L), 0.0)
    Tm = P
    span = 2
    while span < c:
        P = _bdot(P, P)
        Tm = Tm + P + _bdot(Tm, P)
        span *= 2
    rhs = jnp.concatenate([v * beta_col, k * (beta_col * jnp.exp(d_col))], axis=1)
    X = rhs + _bdot(Tm, rhs)
    u, w = X[:, :HEAD_DIM], X[:, HEAD_DIM:]
    v_new = u - _bdot(w, S)
    o = _bdot(q * jnp.exp(d_col), S) + _bdot(qk, v_new)
    d_last = d_col[c - 1:c, :]
    S_new = jnp.exp(d_last) * S + _bdot_tn(k * jnp.exp(d_last - d_col), v_new)
    return o, S_new


def _head_norm(o, w):
    return o * lax.rsqrt(jnp.mean(o * o, axis=-1, keepdims=True) + EPS) * w


def _gdn_gate_tile(ab, alog, dtb):
    lane = lax.broadcasted_iota(jnp.int32, ab.shape, 1)
    g = -jnp.exp(alog) * _softplus(ab + dtb)
    return jnp.where(lane < HEADS, g, _sigmoid(ab))


def _l2norm_heads(y, scale):
    outs = []
    for h in range(y.shape[-1] // HEAD_DIM):
        seg = y[:, h * HEAD_DIM:(h + 1) * HEAD_DIM]
        outs.append(seg * (lax.rsqrt(jnp.sum(seg * seg, axis=-1, keepdims=True) + EPS) * scale))
    return jnp.concatenate(outs, axis=-1)


P_TB = 256
P_G = 2


def _conv_prompt_body(x_ref, w_ref, o_ref, carry_ref):
    t = pl.program_id(2)
    c = pl.program_id(1)

    @pl.when(t == 0)
    def _():
        carry_ref[...] = jnp.zeros_like(carry_ref)

    x = x_ref[...]
    tb = x.shape[0]
    carry = carry_ref[...]
    row8 = lax.broadcasted_iota(jnp.int32, carry.shape, 0)
    y = x * w_ref[CONV_W - 1:CONV_W, :]
    for j in range(1, CONV_W):
        xr = pltpu.roll(x, j, axis=0)
        top = jnp.where(row8 < j, pltpu.roll(carry, j, axis=0), xr[:SUBLANES])
        sh = jnp.concatenate([top, xr[SUBLANES:]], axis=0)
        y = y + sh * w_ref[CONV_W - 1 - j:CONV_W - j, :]
    carry_ref[...] = x[tb - SUBLANES:]
    y = _silu(y)

    @pl.when(c == 0)
    def _():
        o_ref[...] = _l2norm_heads(y, HEAD_DIM ** -0.5)

    @pl.when(c == 1)
    def _():
        o_ref[...] = _l2norm_heads(y, 1.0)

    @pl.when(c == 2)
    def _():
        o_ref[...] = y


def _conv_prompt(proj, conv_w, layer, batch, seq):
    nt = seq // P_TB
    qkv_blk = 4 * WIDTH // WIDTH
    return pl.pallas_call(
        _conv_prompt_body,
        grid=(batch, 3, nt),
        in_specs=[pl.BlockSpec((P_TB, WIDTH), lambda b, c, t: (b * nt + t, qkv_blk + c)),
                  pl.BlockSpec((None, CONV_W, WIDTH), lambda b, c, t: (layer, 0, c))],
        out_specs=pl.BlockSpec((P_TB, WIDTH), lambda b, c, t: (b * nt + t, c)),
        out_shape=jax.ShapeDtypeStruct((batch * seq, CONV_DIM), F32),
        scratch_shapes=[pltpu.VMEM((SUBLANES, WIDTH), F32)],
        compiler_params=_params(("parallel", "parallel", "arbitrary")),
        name="conv_prompt",
    )(proj, conv_w)


def _hgrn_prompt_body(q_ref, f_ref, i_ref, gate_ref, lb_ref, nw_ref, o_ref, s_ref, s_scr):
    t = pl.program_id(2)

    @pl.when(t == 0)
    def _():
        s_scr[...] = jnp.zeros_like(s_scr)

    nw = nw_ref[...]
    for g in range(P_G):
        cols = slice(g * HEAD_DIM, (g + 1) * HEAD_DIM)
        lb = lb_ref[:, cols]
        S = s_scr[g]
        for c in range(P_TB // CHUNK):
            rows = slice(c * CHUNK, (c + 1) * CHUNK)
            q, k, logf = _hgrn_gates(q_ref[rows, cols], f_ref[rows, cols], lb)
            o, S = _hgrn_chunk(q, k, i_ref[rows, cols], logf, S, SUB, CHUNK)
            o = _head_norm(o, nw) * _sigmoid(gate_ref[rows, cols])
            o_ref[rows, cols] = o.astype(o_ref.dtype)
        s_scr[g] = S

    @pl.when(t == pl.num_programs(2) - 1)
    def _():
        s_ref[0] = s_scr[...]


def _hgrn_prompt(proj, lb, hg_norm, layer, batch, seq):
    nt = seq // P_TB
    gw = P_G * HEAD_DIM
    nhb = HEADS // P_G

    def col(region):
        return lambda b, h, t: (b * nt + t, region * nhb + h)

    return pl.pallas_call(
        _hgrn_prompt_body,
        grid=(batch, nhb, nt),
        in_specs=[pl.BlockSpec((P_TB, gw), col(0)),
                  pl.BlockSpec((P_TB, gw), col(1)),
                  pl.BlockSpec((P_TB, gw), col(2)),
                  pl.BlockSpec((P_TB, gw), col(3)),
                  pl.BlockSpec((None, 1, gw), lambda b, h, t: (layer, 0, h)),
                  pl.BlockSpec((None, 1, HEAD_DIM), lambda b, h, t: (layer, 0, 0))],
        out_specs=[pl.BlockSpec((P_TB, gw), lambda b, h, t: (b * nt + t, h)),
                   pl.BlockSpec((1, P_G, HEAD_DIM, HEAD_DIM), lambda b, h, t: (b, h, 0, 0))],
        out_shape=[jax.ShapeDtypeStruct((batch * seq, WIDTH), BF16),
                   jax.ShapeDtypeStruct((batch, HEADS, HEAD_DIM, HEAD_DIM), F32)],
        scratch_shapes=[pltpu.VMEM((P_G, HEAD_DIM, HEAD_DIM), F32)],
        compiler_params=_params(("parallel", "parallel", "arbitrary")),
        name="hgrn_prompt",
    )(proj, proj, proj, proj, lb, hg_norm)


def _gdn_prompt_body(q_ref, k_ref, v_ref, z_ref, ab_ref, alog_ref, dtb_ref, nw_ref,
                     o_ref, s_ref, s_scr):
    hb = pl.program_id(1)
    t = pl.program_id(2)

    @pl.when(t == 0)
    def _():
        s_scr[...] = jnp.zeros_like(s_scr)

    nw = nw_ref[...]
    gb = _gdn_gate_tile(ab_ref[...], alog_ref[...], dtb_ref[...])
    lane = lax.broadcasted_iota(jnp.int32, (CHUNK, LANES), 1)
    S = [s_scr[g] for g in range(P_G)]
    for c in range(P_TB // CHUNK):
        rows = slice(c * CHUNK, (c + 1) * CHUNK)
        gbc = gb[rows]
        d_all = _cumsum_rows(gbc)
        for g in range(P_G):
            cols = slice(g * HEAD_DIM, (g + 1) * HEAD_DIM)
            head = hb * P_G + g
            d_col = jnp.sum(jnp.where(lane == head, d_all, 0.0), axis=1, keepdims=True)
            beta_col = jnp.sum(jnp.where(lane == head + HEADS, gbc, 0.0), axis=1, keepdims=True)
            o, S[g] = _gdn_chunk(q_ref[rows, cols], k_ref[rows, cols], v_ref[rows, cols],
                                 d_col, beta_col, S[g])
            o = _head_norm(o, nw) * _silu(z_ref[rows, cols])
            o_ref[rows, cols] = o.astype(o_ref.dtype)
    for g in range(P_G):
        s_scr[g] = S[g]

    @pl.when(t == pl.num_programs(2) - 1)
    def _():
        s_ref[0] = s_scr[...]


def _gdn_prompt(qkv, proj, ab, alog, dtb, gdn_norm, layer, batch, seq):
    nt = seq // P_TB
    gw = P_G * HEAD_DIM
    nhb = HEADS // P_G
    z_blk = (4 * WIDTH + CONV_DIM) // gw

    def col(region):
        return lambda b, h, t: (b * nt + t, region * nhb + h)

    return pl.pallas_call(
        _gdn_prompt_body,
        grid=(batch, nhb, nt),
        in_specs=[pl.BlockSpec((P_TB, gw), col(0)),
                  pl.BlockSpec((P_TB, gw), col(1)),
                  pl.BlockSpec((P_TB, gw), col(2)),
                  pl.BlockSpec((P_TB, gw), lambda b, h, t: (b * nt + t, z_blk + h)),
                  pl.BlockSpec((P_TB, LANES), lambda b, h, t: (b * nt + t, 0)),
                  pl.BlockSpec((None, 1, LANES), lambda b, h, t: (layer, 0, 0)),
                  pl.BlockSpec((None, 1, LANES), lambda b, h, t: (layer, 0, 0)),
                  pl.BlockSpec((None, 1, HEAD_DIM), lambda b, h, t: (layer, 0, 0))],
        out_specs=[pl.BlockSpec((P_TB, gw), lambda b, h, t: (b * nt + t, h)),
                   pl.BlockSpec((1, P_G, HEAD_DIM, HEAD_DIM), lambda b, h, t: (b, h, 0, 0))],
        out_shape=[jax.ShapeDtypeStruct((batch * seq, WIDTH), BF16),
                   jax.ShapeDtypeStruct((batch, HEADS, HEAD_DIM, HEAD_DIM), F32)],
        scratch_shapes=[pltpu.VMEM((P_G, HEAD_DIM, HEAD_DIM), F32)],
        compiler_params=_params(("parallel", "parallel", "arbitrary")),
        name="gdn_prompt",
    )(qkv, qkv, qkv, proj, ab, alog, dtb, gdn_norm)


S_BB = 8
S_T = 4
S_C = 8


def _conv_sample_body(x_ref, buf_ref, w_ref, o_ref):
    c = pl.program_id(1)
    xx = [buf_ref[:, j, :] for j in range(CONV_W - 1)] + [x_ref[:, t, :] for t in range(S_T)]
    for t in range(S_T):
        y = xx[t] * w_ref[0:1, :]
        for j in range(1, CONV_W):
            y = y + xx[t + j] * w_ref[j:j + 1, :]
        y = _silu(y)
        scale = jnp.where(c == 0, HEAD_DIM ** -0.5, 1.0)
        yn = _l2norm_heads(y, scale)
        o_ref[:, t, :] = jnp.where(c == 2, y, yn)


def _conv_sample(proj3, buf, conv_w, layer):
    nb = proj3.shape[0]
    qkv_blk = 4
    return pl.pallas_call(
        _conv_sample_body,
        grid=(nb // S_BB, 3),
        in_specs=[pl.BlockSpec((S_BB, S_T, WIDTH), lambda i, c: (i, 0, qkv_blk + c)),
                  pl.BlockSpec((None, S_BB, CONV_W - 1, WIDTH), lambda i, c: (layer, i, 0, c)),
                  pl.BlockSpec((None, CONV_W, WIDTH), lambda i, c: (layer, 0, c))],
        out_specs=pl.BlockSpec((S_BB, S_T, WIDTH), lambda i, c: (i, 0, c)),
        out_shape=jax.ShapeDtypeStruct((nb, S_T, CONV_DIM), F32),
        compiler_params=_params(("parallel", "arbitrary")),
        name="conv_sample",
    )(proj3, buf, conv_w)


def _load_padded(dst_ref, src_ref):
    dst_ref[...] = jnp.zeros_like(dst_ref)
    dst_ref[0:S_T, :] = src_ref[0]


def _hgrn_sample_body(q_ref, f_ref, i_ref, gate_ref, lb_ref, nw_ref, s0_ref,
                      o_ref, s_ref, qp, fp, ip, gp, op):
    _load_padded(qp, q_ref)
    _load_padded(fp, f_ref)
    _load_padded(ip, i_ref)
    _load_padded(gp, gate_ref)
    nw = nw_ref[...]
    valid = lax.broadcasted_iota(jnp.int32, (S_C, HEAD_DIM), 0) < S_T
    for h in range(HEADS):
        cols = slice(h * HEAD_DIM, (h + 1) * HEAD_DIM)
        q, k, logf = _hgrn_gates(qp[:, cols], fp[:, cols], lb_ref[:, cols], valid)
        o, S = _hgrn_chunk(q, k, ip[:, cols], logf, s0_ref[0, h], S_C, S_T)
        s_ref[0, h] = S
        op[:, cols] = _head_norm(o, nw) * _sigmoid(gp[:, cols])
    o_ref[0] = op[0:S_T, :]


def _hgrn_sample(proj3, lb, hg_norm, s0, layer):
    nb = proj3.shape[0]
    blk = lambda r: pl.BlockSpec((1, S_T, WIDTH), lambda b: (b, 0, r))
    st = pl.BlockSpec((None, 1, HEADS, HEAD_DIM, HEAD_DIM), lambda b: (layer, b, 0, 0, 0))
    return pl.pallas_call(
        _hgrn_sample_body,
        grid=(nb,),
        in_specs=[blk(0), blk(1), blk(2), blk(3),
                  pl.BlockSpec((None, 1, WIDTH), lambda b: (layer, 0, 0)),
                  pl.BlockSpec((None, 1, HEAD_DIM), lambda b: (layer, 0, 0)),
                  st],
        out_specs=[pl.BlockSpec((1, S_T, WIDTH), lambda b: (b, 0, 0)),
                   pl.BlockSpec((1, HEADS, HEAD_DIM, HEAD_DIM), lambda b: (b, 0, 0, 0))],
        out_shape=[jax.ShapeDtypeStruct((nb, S_T, WIDTH), F32),
                   jax.ShapeDtypeStruct((nb, HEADS, HEAD_DIM, HEAD_DIM), F32)],
        scratch_shapes=[pltpu.VMEM((S_C, WIDTH), F32)] * 5,
        compiler_params=_params(("parallel",)),
        name="hgrn_sample",
    )(proj3, proj3, proj3, proj3, lb, hg_norm, s0)


def _gdn_sample_body(q_ref, k_ref, v_ref, z_ref, ab_ref, alog_ref, dtb_ref, nw_ref, s0_ref,
                     o_ref, s_ref, qp, kp, vp, zp, abp, op):
    _load_padded(qp, q_ref)
    _load_padded(kp, k_ref)
    _load_padded(vp, v_ref)
    _load_padded(zp, z_ref)
    _load_padded(abp, ab_ref)
    nw = nw_ref[...]
    lane = lax.broadcasted_iota(jnp.int32, (S_C, LANES), 1)
    valid = lax.broadcasted_iota(jnp.int32, (S_C, LANES), 0) < S_T
    gb = jnp.where(valid, _gdn_gate_tile(abp[...], alog_ref[...], dtb_ref[...]), 0.0)
    d_all = _cumsum_rows(gb)
    for h in range(HEADS):
        cols = slice(h * HEAD_DIM, (h + 1) * HEAD_DIM)
        d_col = jnp.sum(jnp.where(lane == h, d_all, 0.0), axis=1, keepdims=True)
        beta_col = jnp.sum(jnp.where(lane == h + HEADS, gb, 0.0), axis=1, keepdims=True)
        o, S = _gdn_chunk(qp[:, cols], kp[:, cols], vp[:, cols], d_col, beta_col, s0_ref[0, h])
        s_ref[0, h] = S
        op[:, cols] = _head_norm(o, nw) * _silu(zp[:, cols])
    o_ref[0] = op[0:S_T, :]


def _gdn_sample(qkv3, proj3, ab3, alog, dtb, gdn_norm, s0, layer):
    nb = proj3.shape[0]
    blk = lambda r: pl.BlockSpec((1, S_T, WIDTH), lambda b: (b, 0, r))
    z_blk = (4 * WIDTH + CONV_DIM) // WIDTH
    return pl.pallas_call(
        _gdn_sample_body,
        grid=(nb,),
        in_specs=[blk(0), blk(1), blk(2), blk(z_blk),
                  pl.BlockSpec((1, S_T, LANES), lambda b: (b, 0, 0)),
                  pl.BlockSpec((None, 1, LANES), lambda b: (layer, 0, 0)),
                  pl.BlockSpec((None, 1, LANES), lambda b: (layer, 0, 0)),
                  pl.BlockSpec((None, 1, HEAD_DIM), lambda b: (layer, 0, 0)),
                  pl.BlockSpec((None, 1, HEADS, HEAD_DIM, HEAD_DIM), lambda b: (layer, b, 0, 0, 0))],
        out_specs=[pl.BlockSpec((1, S_T, WIDTH), lambda b: (b, 0, 0)),
                   pl.BlockSpec((1, HEADS, HEAD_DIM, HEAD_DIM), lambda b: (b, 0, 0, 0))],
        out_shape=[jax.ShapeDtypeStruct((nb, S_T, WIDTH), F32),
                   jax.ShapeDtypeStruct((nb, HEADS, HEAD_DIM, HEAD_DIM), F32)],
        scratch_shapes=[pltpu.VMEM((S_C, WIDTH), F32)] * 4 + [pltpu.VMEM((S_C, LANES), F32),
                                                              pltpu.VMEM((S_C, WIDTH), F32)],
        compiler_params=_params(("parallel",)),
        name="gdn_sample",
    )(qkv3, qkv3, qkv3, proj3, ab3, alog, dtb, gdn_norm, s0)


def _trunk(x, sample, states, W):
    rows = x.shape[0]
    tm = 512 if sample else 1024
    hg_states, gdn_states, conv_states = [], [], []
    for l in range(DEPTH):
        xn = _rmsnorm(x, W["ffn1_norm"][l], BF16)
        a = _swiglu_up(xn, W["ffn1_w_gu"], l, tm)
        x = _down_residual(a, W["ffn1_w_down"], l, x, 0.5, tm, 256)

        xn = _rmsnorm(x, W["mix_norm"][l], BF16)
        proj = _matmul(xn, W["w_in_main"], l, tm, 1024)
        ab = _matmul(xn, W["w_in_ab"], l, tm, LANES)
        if sample:
            nb = rows // S_T
            proj3 = proj.reshape(nb, S_T, MAIN_COLS)
            ab3 = ab.reshape(nb, S_T, LANES)
            qkv3 = _conv_sample(proj3, states["conv"], W["conv_w"], l)
            o_hg, s_hg = _hgrn_sample(proj3, W["lb"], W["hg_norm"], states["hgrn"], l)
            o_gdn, s_gdn = _gdn_sample(qkv3, proj3, ab3, W["alog"], W["dtb"], W["gdn_norm"],
                                       states["gdn"], l)
            o_hg = o_hg.reshape(rows, WIDTH)
            o_gdn = o_gdn.reshape(rows, WIDTH)
            conv_new = proj3[:, S_T - (CONV_W - 1):, 4 * WIDTH:4 * WIDTH + CONV_DIM]
        else:
            batch, seq = states["batch"], states["seq"]
            qkv = _conv_prompt(proj, W["conv_w"], l, batch, seq)
            o_hg, s_hg = _hgrn_prompt(proj, W["lb"], W["hg_norm"], l, batch, seq)
            o_gdn, s_gdn = _gdn_prompt(qkv, proj, ab, W["alog"], W["dtb"], W["gdn_norm"],
                                       l, batch, seq)
            conv_new = proj.reshape(batch, seq, MAIN_COLS)[
                :, seq - (CONV_W - 1):, 4 * WIDTH:4 * WIDTH + CONV_DIM]
        x = _mix_out_residual(o_hg, o_gdn, W["w_out"], l, x, tm, 1024)

        xn = _rmsnorm(x, W["ffn2_norm"][l], BF16)
        a = _swiglu_up(xn, W["ffn2_w_gu"], l, tm)
        x = _down_residual(a, W["ffn2_w_down"], l, x, 0.5, tm, 256)

        hg_states.append(s_hg)
        gdn_states.append(s_gdn)
        conv_states.append(conv_new)
    y = _rmsnorm(x, W["final_norm"], F32)
    return y, jnp.stack(hg_states), jnp.stack(gdn_states), jnp.stack(conv_states)


def kernel(x_prompt, x_sample, state_hgrn, state_gdn, state_conv, ffn1_norm, ffn1_w_gu, ffn1_w_down, mix_norm, w_in, hg_lower_bounds, hg_norm, gdn_conv_w, gdn_A_log, gdn_dt_bias, gdn_norm, w_out, ffn2_norm, ffn2_w_gu, ffn2_w_down, final_norm):
    batch, seq, _ = x_prompt.shape
    dec_batch, dec_seq, _ = x_sample.shape
    assert dec_seq == S_T and seq % P_TB == 0

    ab_lo = 4 * WIDTH + CONV_DIM
    row3 = lambda v: v.astype(F32)[:, None, :]
    pad_row = lambda v: row3(jnp.pad(v, ((0, 0), (0, LANES - v.shape[1]))))
    W = {
        "ffn1_norm": ffn1_norm, "mix_norm": mix_norm, "ffn2_norm": ffn2_norm,
        "final_norm": final_norm, "hg_norm": row3(hg_norm), "gdn_norm": row3(gdn_norm),
        "conv_w": gdn_conv_w,
        "ffn1_w_gu": ffn1_w_gu.astype(BF16), "ffn1_w_down": ffn1_w_down.astype(BF16),
        "ffn2_w_gu": ffn2_w_gu.astype(BF16), "ffn2_w_down": ffn2_w_down.astype(BF16),
        "w_out": w_out.astype(BF16),
        "w_in_main": jnp.concatenate([w_in[:, :, :ab_lo], w_in[:, :, ab_lo + 2 * HEADS:]],
                                     axis=-1).astype(BF16),
        "w_in_ab": jnp.pad(w_in[:, :, ab_lo:ab_lo + 2 * HEADS],
                           ((0, 0), (0, 0), (0, LANES - 2 * HEADS))).astype(BF16),
        "alog": pad_row(gdn_A_log), "dtb": pad_row(gdn_dt_bias),
        "lb": row3(_lower_bounds(hg_lower_bounds)),
    }

    y_p, hg_p, gdn_p, conv_p = _trunk(
        x_prompt.reshape(batch * seq, D_MODEL), False, {"batch": batch, "seq": seq}, W)
    y_s, hg_s, gdn_s, conv_s = _trunk(
        x_sample.reshape(dec_batch * dec_seq, D_MODEL), True,
        {"hgrn": state_hgrn, "gdn": state_gdn, "conv": state_conv}, W)
    return (y_p.reshape(batch, seq, D_MODEL), y_s.reshape(dec_batch, dec_seq, D_MODEL),
            hg_p, gdn_p, conv_p, hg_s, gdn_s, conv_s)
```

```python
import functools

import jax
import jax.numpy as jnp
from jax import lax
from jax.experimental import pallas as pl
from jax.experimental.pallas import tpu as pltpu

D_MODEL = 4096
DEPTH = 4
HEAD_DIM = 128
HEADS = 16
WIDTH = HEADS * HEAD_DIM
CONV_W = 4
CONV_DIM = 3 * WIDTH
D_FF = 11008
EPS = 1e-6
F_FLOOR = 1e-20
LOG2E = 1.4426950408889634
LANES = 128
SUBLANES = 8
FF_TILE = 256
CHUNK = 64
SUB = 16
MAIN_COLS = 4 * WIDTH + CONV_DIM
VMEM_LIMIT = 56 * 1024 * 1024

BF16 = jnp.bfloat16
F32 = jnp.float32


def _params(sem, vmem=VMEM_LIMIT):
    return pltpu.CompilerParams(dimension_semantics=sem, vmem_limit_bytes=vmem)


def _bdot(a, b):
    return jnp.dot(a.astype(BF16), b.astype(BF16), preferred_element_type=F32)


def _bdot_nt(a, b):
    return lax.dot_general(a.astype(BF16), b.astype(BF16), (((1,), (1,)), ((), ())),
                           preferred_element_type=F32)


def _bdot_tn(a, b):
    return lax.dot_general(a.astype(BF16), b.astype(BF16), (((0,), (0,)), ((), ())),
                           preferred_element_type=F32)


def _sigmoid(x):
    return 1.0 / (1.0 + jnp.exp(-x))


def _silu(x):
    return x * _sigmoid(x)


def _softplus(x):
    return jnp.maximum(x, 0.0) + jnp.log(1.0 + jnp.exp(-jnp.abs(x)))


def _cumsum_rows(x):
    n = x.shape[0]
    row = lax.broadcasted_iota(jnp.int32, x.shape, 0)
    s = 1
    while s < n:
        x = x + jnp.where(row >= s, pltpu.roll(x, s, axis=0), 0.0)
        s *= 2
    return x


def _row_to_col(r):
    n = r.shape[1]
    ri = lax.broadcasted_iota(jnp.int32, (n, n), 0)
    ci = lax.broadcasted_iota(jnp.int32, (n, n), 1)
    return jnp.sum(jnp.where(ri == ci, jnp.broadcast_to(r, (n, n)), 0.0), axis=1, keepdims=True)


def _col_to_row(c):
    n = c.shape[0]
    ri = lax.broadcasted_iota(jnp.int32, (n, n), 0)
    ci = lax.broadcasted_iota(jnp.int32, (n, n), 1)
    return jnp.sum(jnp.where(ri == ci, jnp.broadcast_to(c, (n, n)), 0.0), axis=0, keepdims=True)


def _rmsnorm_body(x_ref, w_ref, o_ref):
    x = x_ref[...]
    y = x * lax.rsqrt(jnp.mean(x * x, axis=-1, keepdims=True) + EPS)
    o_ref[...] = (y * w_ref[...]).astype(o_ref.dtype)


def _rmsnorm(x, w, out_dtype, tm=256):
    m = x.shape[0]
    return pl.pallas_call(
        _rmsnorm_body,
        grid=(m // tm,),
        in_specs=[pl.BlockSpec((tm, D_MODEL), lambda i: (i, 0)),
                  pl.BlockSpec((1, D_MODEL), lambda i: (0, 0))],
        out_specs=pl.BlockSpec((tm, D_MODEL), lambda i: (i, 0)),
        out_shape=jax.ShapeDtypeStruct((m, D_MODEL), out_dtype),
        compiler_params=_params(("parallel",)),
        name="rmsnorm",
    )(x, w.reshape(1, D_MODEL))


def _mm_body(x_ref, w_ref, o_ref):
    o_ref[...] = jnp.dot(x_ref[...], w_ref[...].astype(BF16), preferred_element_type=F32)


def _matmul(xn, w, layer, tm, tn, n):
    m, k = xn.shape
    return pl.pallas_call(
        _mm_body,
        grid=(m // tm, n // tn),
        in_specs=[pl.BlockSpec((tm, k), lambda i, j: (i, 0)),
                  pl.BlockSpec((None, k, tn), lambda i, j: (layer, 0, j))],
        out_specs=pl.BlockSpec((tm, tn), lambda i, j: (i, j)),
        out_shape=jax.ShapeDtypeStruct((m, n), F32),
        compiler_params=_params(("parallel", "arbitrary")),
        name="matmul",
    )(xn, w)


def _swiglu_body(x_ref, wg_ref, wu_ref, o_ref):
    x = x_ref[...]
    g = jnp.dot(x, wg_ref[...].astype(BF16), preferred_element_type=F32)
    u = jnp.dot(x, wu_ref[...].astype(BF16), preferred_element_type=F32)
    o_ref[...] = (_silu(g) * u).astype(o_ref.dtype)


def _swiglu_up(xn, w_gu, layer, tm):
    m, k = xn.shape
    nj = D_FF // FF_TILE
    return pl.pallas_call(
        _swiglu_body,
        grid=(m // tm, nj),
        in_specs=[pl.BlockSpec((tm, k), lambda i, j: (i, 0)),
                  pl.BlockSpec((None, k, FF_TILE), lambda i, j: (layer, 0, j)),
                  pl.BlockSpec((None, k, FF_TILE), lambda i, j: (layer, 0, j + nj))],
        out_specs=pl.BlockSpec((tm, FF_TILE), lambda i, j: (i, j)),
        out_shape=jax.ShapeDtypeStruct((m, D_FF), BF16),
        compiler_params=_params(("parallel", "arbitrary")),
        name="swiglu_up",
    )(xn, w_gu, w_gu)


def _down_body(scale, a_ref, w_ref, r_ref, o_ref):
    acc = jnp.dot(a_ref[...], w_ref[...], preferred_element_type=F32)
    o_ref[...] = r_ref[...] + scale * acc


def _down_residual(a, w, layer, res, scale, tm, tn):
    m, k = a.shape
    return pl.pallas_call(
        functools.partial(_down_body, scale),
        grid=(m // tm, D_MODEL // tn),
        in_specs=[pl.BlockSpec((tm, k), lambda i, j: (i, 0), pipeline_mode=pl.Buffered(1)),
                  pl.BlockSpec((None, k, tn), lambda i, j: (layer, 0, j)),
                  pl.BlockSpec((tm, tn), lambda i, j: (i, j))],
        out_specs=pl.BlockSpec((tm, tn), lambda i, j: (i, j)),
        out_shape=jax.ShapeDtypeStruct((m, D_MODEL), F32),
        compiler_params=_params(("parallel", "arbitrary")),
        name="down_residual",
    )(a, w, res)


def _mixout_body(a1_ref, a2_ref, w1_ref, w2_ref, r_ref, o_ref):
    acc = jnp.dot(a1_ref[...].astype(BF16), w1_ref[...].astype(BF16), preferred_element_type=F32)
    acc += jnp.dot(a2_ref[...].astype(BF16), w2_ref[...].astype(BF16), preferred_element_type=F32)
    o_ref[...] = r_ref[...] + acc


def _mix_out_residual(o_hg, o_gdn, w_out, layer, res, tm, tn):
    m = o_hg.shape[0]
    return pl.pallas_call(
        _mixout_body,
        grid=(m // tm, D_MODEL // tn),
        in_specs=[pl.BlockSpec((tm, WIDTH), lambda i, j: (i, 0)),
                  pl.BlockSpec((tm, WIDTH), lambda i, j: (i, 0)),
                  pl.BlockSpec((None, WIDTH, tn), lambda i, j: (layer, 0, j)),
                  pl.BlockSpec((None, WIDTH, tn), lambda i, j: (layer, 1, j)),
                  pl.BlockSpec((tm, tn), lambda i, j: (i, j))],
        out_specs=pl.BlockSpec((tm, tn), lambda i, j: (i, j)),
        out_shape=jax.ShapeDtypeStruct((m, D_MODEL), F32),
        compiler_params=_params(("parallel", "arbitrary")),
        name="mix_out_residual",
    )(o_hg, o_gdn, w_out, w_out, res)


def _lb_body(p_ref, o_ref):
    x = p_ref[...]
    e = jnp.exp(x - jnp.max(x, axis=0, keepdims=True))
    p = e / jnp.sum(e, axis=0, keepdims=True)
    rows = [p[0:1]]
    for l in range(1, DEPTH):
        rows.append(rows[-1] + p[l:l + 1])
    cs = jnp.concatenate(rows, axis=0)
    o_ref[...] = jnp.clip(cs - p[0:1], 0.0, 1.0)


def _lower_bounds(hg_lower_bounds):
    return pl.pallas_call(
        _lb_body,
        out_shape=jax.ShapeDtypeStruct((DEPTH, WIDTH), F32),
        name="lower_bounds",
    )(hg_lower_bounds)


def _hgrn_gates(zq, zf, lb, valid=None):
    q = _silu(zq)
    sig = _sigmoid(zf)
    f = lb + (1.0 - lb) * sig
    logf = jnp.log(jnp.maximum(f, F_FLOOR))
    k = (1.0 - lb) * (1.0 - sig)
    if valid is not None:
        logf = jnp.where(valid, logf, 0.0)
        k = jnp.where(valid, k, 0.0)
    return q, k, logf


def _hgrn_diag(qs, ks, bs, vs, n_valid):
    n = qs.shape[0]
    row = lax.broadcasted_iota(jnp.int32, (n, 1), 0)
    out = jnp.zeros_like(vs)
    for s in range(n_valid):
        mask = row >= s
        diff = jnp.where(mask, bs - bs[s:s + 1, :], 0.0)
        col = jnp.sum(qs * ks[s:s + 1, :] * jnp.exp(diff), axis=-1, keepdims=True)
        out = out + jnp.where(mask, col, 0.0) * vs[s:s + 1, :]
    return out


def _hgrn_chunk(q, k, v, logf, S, sub, n_valid):
    c = q.shape[0]
    b = _cumsum_rows(logf)
    o_inter = _bdot(q * jnp.exp(b), S)
    outs = []
    for i in range(c // sub):
        lo, hi = i * sub, (i + 1) * sub
        qs, ks, bs, vs = q[lo:hi], k[lo:hi], b[lo:hi], v[lo:hi]
        oi = o_inter[lo:hi] + _hgrn_diag(qs, ks, bs, vs, min(sub, max(n_valid - lo, 0)))
        if i > 0:
            ref = b[lo - 1:lo, :]
            att = _bdot_nt(qs * jnp.exp(bs - ref), k[:lo] * jnp.exp(ref - b[:lo]))
            oi = oi + _bdot(att, v[:lo])
        outs.append(oi)
    o = outs[0] if len(outs) == 1 else jnp.concatenate(outs, axis=0)
    bl = b[c - 1:c, :]
    upd = _bdot_tn(k * jnp.exp(bl - b), v)
    S_new = _row_to_col(jnp.exp(bl)) * S + upd
    return o, S_new


def _gdn_wy(qs, ks, vs, d_cols, d_rows, beta_cols):
    c = qs[0].shape[0]
    n = range(len(qs))
    ri = lax.broadcasted_iota(jnp.int32, (c, c), 0)
    ci = lax.broadcasted_iota(jnp.int32, (c, c), 1)
    causal = ri >= ci
    strict = ri > ci
    Ls = [jnp.where(causal, jnp.exp(jnp.where(causal, d_cols[i] - d_rows[i], 0.0)), 0.0) for i in n]
    kks = [_bdot_nt(ks[i], ks[i]) for i in n]
    qks = [_bdot_nt(qs[i], ks[i]) for i in n]
    Ps = [jnp.where(strict, -(beta_cols[i] * kks[i] * Ls[i]), 0.0) for i in n]
    Tms = Ps
    span = 2
    while span < c:
        Ps = [_bdot(Ps[i], Ps[i]) for i in n]
        TPs = [_bdot(Tms[i], Ps[i]) for i in n]
        Tms = [Tms[i] + Ps[i] + TPs[i] for i in n]
        span *= 2
    eds = [jnp.exp(d_cols[i]) for i in n]
    rhss = [jnp.concatenate([vs[i] * beta_cols[i], ks[i] * (beta_cols[i] * eds[i])], axis=1)
            for i in n]
    Xs = [rhss[i] + _bdot(Tms[i], rhss[i]) for i in n]
    out = []
    for i in n:
        d_last = d_cols[i][c - 1:c, :]
        out.append(dict(u=Xs[i][:, :HEAD_DIM], w=Xs[i][:, HEAD_DIM:], qkl=qks[i] * Ls[i],
                        qe=qs[i] * eds[i], kdec=ks[i] * jnp.exp(d_last - d_cols[i]),
                        e_last=jnp.exp(d_last)))
    return out


def _gdn_apply(wy, Ss):
    n = range(len(wy))
    c = wy[0]["u"].shape[0]
    sws = [_bdot(jnp.concatenate([wy[i]["w"], wy[i]["qe"]], axis=0), Ss[i]) for i in n]
    v_news = [wy[i]["u"] - sws[i][:c] for i in n]
    o2s = [_bdot(wy[i]["qkl"], v_news[i]) for i in n]
    upds = [_bdot_tn(wy[i]["kdec"], v_news[i]) for i in n]
    outs = [sws[i][c:] + o2s[i] for i in n]
    return outs, [wy[i]["e_last"] * Ss[i] + upds[i] for i in n]


def _head_norm(o, w):
    return o * lax.rsqrt(jnp.mean(o * o, axis=-1, keepdims=True) + EPS) * w


def _gdn_gate_tile(ab, alog, dtb):
    lane = lax.broadcasted_iota(jnp.int32, ab.shape, 1)
    g = -jnp.exp(alog) * _softplus(ab + dtb)
    return jnp.where(lane < HEADS, g, _sigmoid(ab))


def _l2norm_heads(y, scale):
    outs = []
    for h in range(y.shape[-1] // HEAD_DIM):
        seg = y[:, h * HEAD_DIM:(h + 1) * HEAD_DIM]
        outs.append(seg * (lax.rsqrt(jnp.sum(seg * seg, axis=-1, keepdims=True) + EPS) * scale))
    return jnp.concatenate(outs, axis=-1)


P_TB = 256
H_TB = 128
G_TB = 128
G_HG = 8


def _conv_prompt_body(x_ref, w_ref, o_ref, carry_ref):
    t = pl.program_id(2)
    c = pl.program_id(1)

    @pl.when(t == 0)
    def _():
        carry_ref[...] = jnp.zeros_like(carry_ref)

    x = x_ref[...]
    tb = x.shape[0]
    carry = carry_ref[...]
    row8 = lax.broadcasted_iota(jnp.int32, carry.shape, 0)
    y = x * w_ref[CONV_W - 1:CONV_W, :]
    for j in range(1, CONV_W):
        xr = pltpu.roll(x, j, axis=0)
        top = jnp.where(row8 < j, pltpu.roll(carry, j, axis=0), xr[:SUBLANES])
        sh = jnp.concatenate([top, xr[SUBLANES:]], axis=0)
        y = y + sh * w_ref[CONV_W - 1 - j:CONV_W - j, :]
    carry_ref[...] = x[tb - SUBLANES:]
    y = _silu(y)

    @pl.when(c == 0)
    def _():
        o_ref[...] = _l2norm_heads(y, HEAD_DIM ** -0.5)

    @pl.when(c == 1)
    def _():
        o_ref[...] = _l2norm_heads(y, 1.0)

    @pl.when(c == 2)
    def _():
        o_ref[...] = y


def _conv_prompt(proj, conv_w, layer, batch, seq):
    nt = seq // P_TB
    qkv_blk = 4 * WIDTH // WIDTH
    return pl.pallas_call(
        _conv_prompt_body,
        grid=(batch, 3, nt),
        in_specs=[pl.BlockSpec((P_TB, WIDTH), lambda b, c, t: (b * nt + t, qkv_blk + c)),
                  pl.BlockSpec((None, CONV_W, WIDTH), lambda b, c, t: (layer, 0, c))],
        out_specs=pl.BlockSpec((P_TB, WIDTH), lambda b, c, t: (b * nt + t, c)),
        out_shape=jax.ShapeDtypeStruct((batch * seq, CONV_DIM), F32),
        scratch_shapes=[pltpu.VMEM((SUBLANES, WIDTH), F32)],
        compiler_params=_params(("parallel", "parallel", "arbitrary")),
        name="conv_prompt",
    )(proj, conv_w)


def _hgrn_prompt_chunk(zq, zf, v, gate, lb, nw, s_scr, k_scr, b_scr):
    heads = range(HEADS)
    hc = [slice(h * HEAD_DIM, (h + 1) * HEAD_DIM) for h in heads]
    band = SUBLANES
    q, k, logf = _hgrn_gates(zq, zf, lb)
    b = _cumsum_rows(logf) * LOG2E
    k_scr[...] = k
    b_scr[...] = b
    bl = b[CHUNK - 1:CHUNK, :]
    qe = q * jnp.exp2(b)
    kdec = k * jnp.exp2(bl - b)
    e_last = jnp.exp2(bl)

    lane = lax.broadcasted_iota(jnp.int32, (band, CHUNK), 1)
    row = lax.broadcasted_iota(jnp.int32, (band, CHUNK), 0)
    a_bands = [[None] * (CHUNK // band) for _ in heads]
    for i in range(CHUNK // SUB):
        base = i * SUB
        q_lo, q_hi = q[base:base + band], q[base + band:base + SUB]
        b_lo, b_hi = b[base:base + band], b[base + band:base + SUB]
        if i == 0:
            a_lo = [jnp.zeros((band, CHUNK), F32) for _ in heads]
            a_hi = [jnp.zeros((band, CHUNK), F32) for _ in heads]
        else:
            ref = b[base - 1:base, :]
            qt = q[base:base + SUB] * jnp.exp2(b[base:base + SUB] - ref)
            kt = k * jnp.exp2(jnp.minimum(ref - b, 0.0))
            att = [jnp.where(lane[:1] < base, _bdot_nt(qt[:, hc[h]], kt[:, hc[h]]), 0.0)
                   for h in heads]
            a_lo = [att[h][:band] for h in heads]
            a_hi = [att[h][band:] for h in heads]
        for s in range(SUB):
            r = base + s
            k_s = k_scr[r:r + 1, :]
            b_s = b_scr[r:r + 1, :]
            p_hi = q_hi * k_s * jnp.exp2(jnp.minimum(b_hi - b_s, 0.0))
            if s < band:
                p_lo = q_lo * k_s * jnp.exp2(jnp.minimum(b_lo - b_s, 0.0))
                m_lo = (lane == r) & (row >= s)
                m_hi = lane == r
            else:
                m_hi = (lane == r) & (row >= s - band)
            for h in heads:
                if s < band:
                    a_lo[h] = jnp.where(m_lo, jnp.sum(p_lo[:, hc[h]], axis=-1, keepdims=True),
                                        a_lo[h])
                a_hi[h] = jnp.where(m_hi, jnp.sum(p_hi[:, hc[h]], axis=-1, keepdims=True),
                                    a_hi[h])
        for h in heads:
            a_bands[h][2 * i] = a_lo[h]
            a_bands[h][2 * i + 1] = a_hi[h]

    states = [s_scr[h] for h in heads]
    o_inter = [_bdot(qe[:, hc[h]], states[h]) for h in heads]
    o_intra = [_bdot(jnp.concatenate(a_bands[h], axis=0), v[:, hc[h]]) for h in heads]
    upd = [_bdot_tn(kdec[:, hc[h]], v[:, hc[h]]) for h in heads]
    outs = []
    for h in heads:
        s_scr[h] = _row_to_col(e_last[:, hc[h]]) * states[h] + upd[h]
        outs.append(_head_norm(o_inter[h] + o_intra[h], nw))
    return jnp.concatenate(outs, axis=1) * _sigmoid(gate)


def _hgrn_prompt_body(q_ref, f_ref, i_ref, gate_ref, lb_ref, nw_ref, o_ref, s_ref,
                      s_scr, k_scr, b_scr):
    t = pl.program_id(1)

    @pl.when(t == 0)
    def _():
        s_scr[...] = jnp.zeros_like(s_scr)

    lb = lb_ref[...]
    nw = nw_ref[...]
    for c in range(H_TB // CHUNK):
        rows = slice(c * CHUNK, (c + 1) * CHUNK)
        o = _hgrn_prompt_chunk(q_ref[rows, :], f_ref[rows, :], i_ref[rows, :], gate_ref[rows, :],
                               lb, nw, s_scr, k_scr, b_scr)
        o_ref[rows, :] = o.astype(o_ref.dtype)

    @pl.when(t == pl.num_programs(1) - 1)
    def _():
        s_ref[0] = s_scr[...]


def _hgrn_prompt(proj, lb, hg_norm, layer, batch, seq):
    nt = seq // H_TB
    blk = lambda r: pl.BlockSpec((H_TB, WIDTH), lambda b, t: (b * nt + t, r))
    return pl.pallas_call(
        _hgrn_prompt_body,
        grid=(batch, nt),
        in_specs=[blk(0), blk(1), blk(2), blk(3),
                  pl.BlockSpec((None, 1, WIDTH), lambda b, t: (layer, 0, 0)),
                  pl.BlockSpec((None, 1, HEAD_DIM), lambda b, t: (layer, 0, 0))],
        out_specs=[pl.BlockSpec((H_TB, WIDTH), lambda b, t: (b * nt + t, 0)),
                   pl.BlockSpec((1, HEADS, HEAD_DIM, HEAD_DIM), lambda b, t: (b, 0, 0, 0))],
        out_shape=[jax.ShapeDtypeStruct((batch * seq, WIDTH), BF16),
                   jax.ShapeDtypeStruct((batch, HEADS, HEAD_DIM, HEAD_DIM), F32)],
        scratch_shapes=[pltpu.VMEM((HEADS, HEAD_DIM, HEAD_DIM), F32),
                        pltpu.VMEM((CHUNK, WIDTH), F32),
                        pltpu.VMEM((CHUNK, WIDTH), F32)],
        compiler_params=_params(("parallel", "arbitrary")),
        name="hgrn_prompt",
    )(proj, proj, proj, proj, lb, hg_norm)


def _gdn_prompt_body(q_ref, k_ref, v_ref, z_ref, ab_ref, alog_ref, dtb_ref, nw_ref,
                     o_ref, s_ref, s_scr):
    t = pl.program_id(1)

    @pl.when(t == 0)
    def _():
        s_scr[...] = jnp.zeros_like(s_scr)

    nw = nw_ref[...]
    gb = _gdn_gate_tile(ab_ref[...], alog_ref[...], dtb_ref[...])
    nc = G_TB // CHUNK
    d_all = jnp.concatenate(
        [_cumsum_rows(gb[c * CHUNK:(c + 1) * CHUNK]) for c in range(nc)], axis=0)
    d_t = d_all.T
    for c in range(nc):
        rows = slice(c * CHUNK, (c + 1) * CHUNK)
        for h0 in range(0, HEADS, G_HG):
            heads = list(range(h0, h0 + G_HG))
            cols = [slice(h * HEAD_DIM, (h + 1) * HEAD_DIM) for h in heads]
            wy = _gdn_wy([q_ref[rows, cs] for cs in cols],
                         [k_ref[rows, cs] for cs in cols],
                         [v_ref[rows, cs] for cs in cols],
                         [d_all[rows, h:h + 1] for h in heads],
                         [d_t[h:h + 1, rows] for h in heads],
                         [gb[rows, HEADS + h:HEADS + h + 1] for h in heads])
            outs, new_states = _gdn_apply(wy, [s_scr[h] for h in heads])
            for i, h in enumerate(heads):
                s_scr[h] = new_states[i]
                o = _head_norm(outs[i], nw) * _silu(z_ref[rows, cols[i]])
                o_ref[rows, cols[i]] = o.astype(o_ref.dtype)

    @pl.when(t == pl.num_programs(1) - 1)
    def _():
        s_ref[0] = s_scr[...]


def _gdn_prompt(qkv, z, ab, alog, dtb, gdn_norm, layer, batch, seq):
    nt = seq // G_TB
    blk = lambda r: pl.BlockSpec((G_TB, WIDTH), lambda b, t: (b * nt + t, r))
    row = lambda n: pl.BlockSpec((None, 1, n), lambda b, t: (layer, 0, 0))
    return pl.pallas_call(
        _gdn_prompt_body,
        grid=(batch, nt),
        in_specs=[blk(0), blk(1), blk(2), blk(0),
                  pl.BlockSpec((G_TB, LANES), lambda b, t: (b * nt + t, 0)),
                  row(LANES), row(LANES), row(HEAD_DIM)],
        out_specs=[pl.BlockSpec((G_TB, WIDTH), lambda b, t: (b * nt + t, 0)),
                   pl.BlockSpec((1, HEADS, HEAD_DIM, HEAD_DIM), lambda b, t: (b, 0, 0, 0))],
        out_shape=[jax.ShapeDtypeStruct((batch * seq, WIDTH), BF16),
                   jax.ShapeDtypeStruct((batch, HEADS, HEAD_DIM, HEAD_DIM), F32)],
        scratch_shapes=[pltpu.VMEM((HEADS, HEAD_DIM, HEAD_DIM), F32)],
        compiler_params=_params(("parallel", "arbitrary")),
        name="gdn_prompt",
    )(qkv, qkv, qkv, z, ab, alog, dtb, gdn_norm)


S_BB = 8
S_T = 4
S_C = 8


def _conv_sample_body(x_ref, buf_ref, w_ref, o_ref):
    c = pl.program_id(1)
    xx = [buf_ref[:, j, :] for j in range(CONV_W - 1)] + [x_ref[:, t, :] for t in range(S_T)]
    for t in range(S_T):
        y = xx[t] * w_ref[0:1, :]
        for j in range(1, CONV_W):
            y = y + xx[t + j] * w_ref[j:j + 1, :]
        y = _silu(y)
        scale = jnp.where(c == 0, HEAD_DIM ** -0.5, 1.0)
        yn = _l2norm_heads(y, scale)
        o_ref[:, t, :] = jnp.where(c == 2, y, yn)


def _conv_sample(proj3, buf, conv_w, layer):
    nb = proj3.shape[0]
    qkv_blk = 4
    return pl.pallas_call(
        _conv_sample_body,
        grid=(nb // S_BB, 3),
        in_specs=[pl.BlockSpec((S_BB, S_T, WIDTH), lambda i, c: (i, 0, qkv_blk + c)),
                  pl.BlockSpec((None, S_BB, CONV_W - 1, WIDTH), lambda i, c: (layer, i, 0, c)),
                  pl.BlockSpec((None, CONV_W, WIDTH), lambda i, c: (layer, 0, c))],
        out_specs=pl.BlockSpec((S_BB, S_T, WIDTH), lambda i, c: (i, 0, c)),
        out_shape=jax.ShapeDtypeStruct((nb, S_T, CONV_DIM), F32),
        compiler_params=_params(("parallel", "arbitrary")),
        name="conv_sample",
    )(proj3, buf, conv_w)


def _load_padded(dst_ref, src_ref):
    dst_ref[...] = jnp.zeros_like(dst_ref)
    dst_ref[0:S_T, :] = src_ref[0]


def _hgrn_sample_body(q_ref, f_ref, i_ref, gate_ref, lb_ref, nw_ref, s0_ref,
                      o_ref, s_ref, qp, fp, ip, gp, op):
    _load_padded(qp, q_ref)
    _load_padded(fp, f_ref)
    _load_padded(ip, i_ref)
    _load_padded(gp, gate_ref)
    nw = nw_ref[...]
    valid = lax.broadcasted_iota(jnp.int32, (S_C, HEAD_DIM), 0) < S_T
    for h in range(HEADS):
        cols = slice(h * HEAD_DIM, (h + 1) * HEAD_DIM)
        q, k, logf = _hgrn_gates(qp[:, cols], fp[:, cols], lb_ref[:, cols], valid)
        o, S = _hgrn_chunk(q, k, ip[:, cols], logf, s0_ref[0, h], S_C, S_T)
        s_ref[0, h] = S
        op[:, cols] = _head_norm(o, nw) * _sigmoid(gp[:, cols])
    o_ref[0] = op[0:S_T, :]


def _hgrn_sample(proj3, lb, hg_norm, s0, layer):
    nb = proj3.shape[0]
    blk = lambda r: pl.BlockSpec((1, S_T, WIDTH), lambda b: (b, 0, r))
    st = pl.BlockSpec((None, 1, HEADS, HEAD_DIM, HEAD_DIM), lambda b: (layer, b, 0, 0, 0))
    return pl.pallas_call(
        _hgrn_sample_body,
        grid=(nb,),
        in_specs=[blk(0), blk(1), blk(2), blk(3),
                  pl.BlockSpec((None, 1, WIDTH), lambda b: (layer, 0, 0)),
                  pl.BlockSpec((None, 1, HEAD_DIM), lambda b: (layer, 0, 0)),
                  st],
        out_specs=[pl.BlockSpec((1, S_T, WIDTH), lambda b: (b, 0, 0)),
                   pl.BlockSpec((1, HEADS, HEAD_DIM, HEAD_DIM), lambda b: (b, 0, 0, 0))],
        out_shape=[jax.ShapeDtypeStruct((nb, S_T, WIDTH), F32),
                   jax.ShapeDtypeStruct((nb, HEADS, HEAD_DIM, HEAD_DIM), F32)],
        scratch_shapes=[pltpu.VMEM((S_C, WIDTH), F32)] * 5,
        compiler_params=_params(("parallel",)),
        name="hgrn_sample",
    )(proj3, proj3, proj3, proj3, lb, hg_norm, s0)


def _gdn_sample_body(q_ref, k_ref, v_ref, z_ref, ab_ref, alog_ref, dtb_ref, nw_ref, s0_ref,
                     o_ref, s_ref, qp, kp, vp, zp, abp, op):
    _load_padded(qp, q_ref)
    _load_padded(kp, k_ref)
    _load_padded(vp, v_ref)
    _load_padded(zp, z_ref)
    _load_padded(abp, ab_ref)
    nw = nw_ref[...]
    valid = lax.broadcasted_iota(jnp.int32, (S_C, LANES), 0) < S_T
    gb = jnp.where(valid, _gdn_gate_tile(abp[...], alog_ref[...], dtb_ref[...]), 0.0)
    d_all = _cumsum_rows(gb)
    row = lax.broadcasted_iota(jnp.int32, (S_C, 1), 0)
    heads = range(HEADS)
    cols = [slice(h * HEAD_DIM, (h + 1) * HEAD_DIM) for h in heads]
    wy = []
    for h in heads:
        q, k, v = qp[:, cols[h]], kp[:, cols[h]], vp[:, cols[h]]
        d = d_all[:, h:h + 1]
        beta = gb[:, HEADS + h:HEADS + h + 1]
        ed = jnp.exp(d)
        X = jnp.concatenate([v * beta, k * (beta * ed)], axis=1)
        qkl = []
        for s in range(S_T):
            k_s = k[s:s + 1, :]
            decay = jnp.where(row >= s, jnp.exp(jnp.where(row >= s, d - d[s:s + 1, :], 0.0)), 0.0)
            qkl.append(jnp.sum(q * k_s, axis=-1, keepdims=True) * decay)
            if s < S_T - 1:
                kk_s = jnp.sum(k * k_s, axis=-1, keepdims=True)
                X = X - jnp.where(row > s, beta * kk_s * decay, 0.0) * X[s:s + 1, :]
        d_last = d[S_T - 1:S_T, :]
        wy.append(dict(u=X[:, :HEAD_DIM], w=X[:, HEAD_DIM:], qe=q * ed, qkl=qkl,
                       kdec=k * jnp.exp(d_last - d), e_last=jnp.exp(d_last)))
    s0 = [s0_ref[0, h] for h in heads]
    sws = [_bdot(jnp.concatenate([wy[h]["w"], wy[h]["qe"]], axis=0), s0[h]) for h in heads]
    v_news = [wy[h]["u"] - sws[h][:S_C] for h in heads]
    upds = [_bdot_tn(wy[h]["kdec"], v_news[h]) for h in heads]
    for h in heads:
        o = sws[h][S_C:]
        for s in range(S_T):
            o = o + wy[h]["qkl"][s] * v_news[h][s:s + 1, :]
        s_ref[0, h] = wy[h]["e_last"] * s0[h] + upds[h]
        op[:, cols[h]] = _head_norm(o, nw) * _silu(zp[:, cols[h]])
    o_ref[0] = op[0:S_T, :]


def _gdn_sample(qkv3, z3, ab3, alog, dtb, gdn_norm, s0, layer):
    nb = qkv3.shape[0]
    blk = lambda r: pl.BlockSpec((1, S_T, WIDTH), lambda b: (b, 0, r))
    return pl.pallas_call(
        _gdn_sample_body,
        grid=(nb,),
        in_specs=[blk(0), blk(1), blk(2), blk(0),
                  pl.BlockSpec((1, S_T, LANES), lambda b: (b, 0, 0)),
                  pl.BlockSpec((None, 1, LANES), lambda b: (layer, 0, 0)),
                  pl.BlockSpec((None, 1, LANES), lambda b: (layer, 0, 0)),
                  pl.BlockSpec((None, 1, HEAD_DIM), lambda b: (layer, 0, 0)),
                  pl.BlockSpec((None, 1, HEADS, HEAD_DIM, HEAD_DIM), lambda b: (layer, b, 0, 0, 0))],
        out_specs=[pl.BlockSpec((1, S_T, WIDTH), lambda b: (b, 0, 0)),
                   pl.BlockSpec((1, HEADS, HEAD_DIM, HEAD_DIM), lambda b: (b, 0, 0, 0))],
        out_shape=[jax.ShapeDtypeStruct((nb, S_T, WIDTH), F32),
                   jax.ShapeDtypeStruct((nb, HEADS, HEAD_DIM, HEAD_DIM), F32)],
        scratch_shapes=[pltpu.VMEM((S_C, WIDTH), F32)] * 4 + [pltpu.VMEM((S_C, LANES), F32),
                                                              pltpu.VMEM((S_C, WIDTH), F32)],
        compiler_params=_params(("parallel",)),
        name="gdn_sample",
    )(qkv3, qkv3, qkv3, z3, ab3, alog, dtb, gdn_norm, s0)


def _trunk(x, sample, states, W):
    rows = x.shape[0]
    tm = 512 if sample else 1024
    hg_states, gdn_states, conv_states = [], [], []
    for l in range(DEPTH):
        xn = _rmsnorm(x, W["ffn1_norm"][l], BF16)
        a = _swiglu_up(xn, W["ffn1_w_gu"], l, tm)
        x = _down_residual(a, W["ffn1_w_down"], l, x, 0.5, tm, 256)

        xn = _rmsnorm(x, W["mix_norm"][l], BF16)
        proj = _matmul(xn, W["w_in"], l, tm, 512, MAIN_COLS)
        z = _matmul(xn, W["w_in_z"], l, tm, 1024, WIDTH)
        ab = _matmul(xn, W["w_in_ab"], l, tm, LANES, LANES)
        if sample:
            nb = rows // S_T
            proj3 = proj.reshape(nb, S_T, MAIN_COLS)
            ab3 = ab.reshape(nb, S_T, LANES)
            qkv3 = _conv_sample(proj3, states["conv"], W["conv_w"], l)
            o_hg, s_hg = _hgrn_sample(proj3, W["lb"], W["hg_norm"], states["hgrn"], l)
            o_gdn, s_gdn = _gdn_sample(qkv3, z.reshape(nb, S_T, WIDTH), ab3, W["alog"], W["dtb"],
                                       W["gdn_norm"], states["gdn"], l)
            o_hg = o_hg.reshape(rows, WIDTH)
            o_gdn = o_gdn.reshape(rows, WIDTH)
            conv_new = proj3[:, S_T - (CONV_W - 1):, 4 * WIDTH:4 * WIDTH + CONV_DIM]
        else:
            batch, seq = states["batch"], states["seq"]
            qkv = _conv_prompt(proj, W["conv_w"], l, batch, seq)
            o_hg, s_hg = _hgrn_prompt(proj, W["lb"], W["hg_norm"], l, batch, seq)
            o_gdn, s_gdn = _gdn_prompt(qkv, z, ab, W["alog"], W["dtb"], W["gdn_norm"],
                                       l, batch, seq)
            conv_new = proj.reshape(batch, seq, MAIN_COLS)[
                :, seq - (CONV_W - 1):, 4 * WIDTH:4 * WIDTH + CONV_DIM]
        x = _mix_out_residual(o_hg, o_gdn, W["w_out"], l, x, tm, 512)

        xn = _rmsnorm(x, W["ffn2_norm"][l], BF16)
        a = _swiglu_up(xn, W["ffn2_w_gu"], l, tm)
        x = _down_residual(a, W["ffn2_w_down"], l, x, 0.5, tm, 256)

        hg_states.append(s_hg)
        gdn_states.append(s_gdn)
        conv_states.append(conv_new)
    y = _rmsnorm(x, W["final_norm"], F32)
    return y, jnp.stack(hg_states), jnp.stack(gdn_states), jnp.stack(conv_states)


def kernel(x_prompt, x_sample, state_hgrn, state_gdn, state_conv, ffn1_norm, ffn1_w_gu, ffn1_w_down, mix_norm, w_in, hg_lower_bounds, hg_norm, gdn_conv_w, gdn_A_log, gdn_dt_bias, gdn_norm, w_out, ffn2_norm, ffn2_w_gu, ffn2_w_down, final_norm):
    batch, seq, _ = x_prompt.shape
    dec_batch, dec_seq, _ = x_sample.shape
    assert dec_seq == S_T and seq % P_TB == 0

    ab_lo = 4 * WIDTH + CONV_DIM
    row3 = lambda v: v.astype(F32)[:, None, :]
    pad_row = lambda v: row3(jnp.pad(v, ((0, 0), (0, LANES - v.shape[1]))))
    W = {
        "ffn1_norm": ffn1_norm, "mix_norm": mix_norm, "ffn2_norm": ffn2_norm,
        "final_norm": final_norm, "hg_norm": row3(hg_norm), "gdn_norm": row3(gdn_norm),
        "conv_w": gdn_conv_w,
        "ffn1_w_gu": ffn1_w_gu, "ffn1_w_down": ffn1_w_down.astype(BF16),
        "ffn2_w_gu": ffn2_w_gu, "ffn2_w_down": ffn2_w_down.astype(BF16),
        "w_out": w_out,
        "w_in": w_in,
        "w_in_z": w_in[:, :, ab_lo + 2 * HEADS:].astype(BF16),
        "w_in_ab": jnp.pad(w_in[:, :, ab_lo:ab_lo + 2 * HEADS],
                           ((0, 0), (0, 0), (0, LANES - 2 * HEADS))).astype(BF16),
        "alog": pad_row(gdn_A_log), "dtb": pad_row(gdn_dt_bias),
        "lb": row3(_lower_bounds(hg_lower_bounds)),
    }

    y_p, hg_p, gdn_p, conv_p = _trunk(
        x_prompt.reshape(batch * seq, D_MODEL), False, {"batch": batch, "seq": seq}, W)
    y_s, hg_s, gdn_s, conv_s = _trunk(
        x_sample.reshape(dec_batch * dec_seq, D_MODEL), True,
        {"hgrn": state_hgrn, "gdn": state_gdn, "conv": state_conv}, W)
    return (y_p.reshape(batch, seq, D_MODEL), y_s.reshape(dec_batch, dec_seq, D_MODEL),
            hg_p, gdn_p, conv_p, hg_s, gdn_s, conv_s)
```

```python
import functools

import jax
import jax.numpy as jnp
from jax import lax
from jax.experimental import pallas as pl
from jax.experimental.pallas import tpu as pltpu

D_MODEL = 4096
DEPTH = 4
HEAD_DIM = 128
HEADS = 16
WIDTH = HEADS * HEAD_DIM
CONV_W = 4
CONV_DIM = 3 * WIDTH
D_FF = 11008
EPS = 1e-6
F_FLOOR = 1e-20
LOG2E = 1.4426950408889634
LANES = 128
SUBLANES = 8
FF_TILE = 256
CHUNK = 64
SUB = 16
MAIN_COLS = 4 * WIDTH + CONV_DIM
VMEM_LIMIT = 56 * 1024 * 1024

BF16 = jnp.bfloat16
F32 = jnp.float32


def _params(sem, vmem=VMEM_LIMIT):
    return pltpu.CompilerParams(dimension_semantics=sem, vmem_limit_bytes=vmem)


def _bdot(a, b):
    return jnp.dot(a.astype(BF16), b.astype(BF16), preferred_element_type=F32)


def _bdot_nt(a, b):
    return lax.dot_general(a.astype(BF16), b.astype(BF16), (((1,), (1,)), ((), ())),
                           preferred_element_type=F32)


def _bdot_tn(a, b):
    return lax.dot_general(a.astype(BF16), b.astype(BF16), (((0,), (0,)), ((), ())),
                           preferred_element_type=F32)


def _sigmoid(x):
    return 1.0 / (1.0 + jnp.exp(-x))


def _silu(x):
    return x * _sigmoid(x)


def _softplus(x):
    return jnp.maximum(x, 0.0) + jnp.log(1.0 + jnp.exp(-jnp.abs(x)))


def _cumsum_rows(x):
    n = x.shape[0]
    row = lax.broadcasted_iota(jnp.int32, x.shape, 0)
    s = 1
    while s < n:
        x = x + jnp.where(row >= s, pltpu.roll(x, s, axis=0), 0.0)
        s *= 2
    return x


def _row_to_col(r):
    n = r.shape[1]
    ri = lax.broadcasted_iota(jnp.int32, (n, n), 0)
    ci = lax.broadcasted_iota(jnp.int32, (n, n), 1)
    return jnp.sum(jnp.where(ri == ci, jnp.broadcast_to(r, (n, n)), 0.0), axis=1, keepdims=True)


def _col_to_row(c):
    n = c.shape[0]
    ri = lax.broadcasted_iota(jnp.int32, (n, n), 0)
    ci = lax.broadcasted_iota(jnp.int32, (n, n), 1)
    return jnp.sum(jnp.where(ri == ci, jnp.broadcast_to(c, (n, n)), 0.0), axis=0, keepdims=True)


def _rmsnorm_body(x_ref, w_ref, o_ref):
    x = x_ref[...]
    y = x * lax.rsqrt(jnp.mean(x * x, axis=-1, keepdims=True) + EPS)
    o_ref[...] = (y * w_ref[...]).astype(o_ref.dtype)


def _rmsnorm(x, w, out_dtype, tm=256):
    m = x.shape[0]
    return pl.pallas_call(
        _rmsnorm_body,
        grid=(m // tm,),
        in_specs=[pl.BlockSpec((tm, D_MODEL), lambda i: (i, 0)),
                  pl.BlockSpec((1, D_MODEL), lambda i: (0, 0))],
        out_specs=pl.BlockSpec((tm, D_MODEL), lambda i: (i, 0)),
        out_shape=jax.ShapeDtypeStruct((m, D_MODEL), out_dtype),
        compiler_params=_params(("parallel",)),
        name="rmsnorm",
    )(x, w.reshape(1, D_MODEL))


def _mm_body(x_ref, w_ref, o_ref):
    o_ref[...] = jnp.dot(x_ref[...], w_ref[...].astype(BF16), preferred_element_type=F32)


def _matmul(xn, w, layer, tm, tn, n):
    m, k = xn.shape
    return pl.pallas_call(
        _mm_body,
        grid=(m // tm, n // tn),
        in_specs=[pl.BlockSpec((tm, k), lambda i, j: (i, 0)),
                  pl.BlockSpec((None, k, tn), lambda i, j: (layer, 0, j))],
        out_specs=pl.BlockSpec((tm, tn), lambda i, j: (i, j)),
        out_shape=jax.ShapeDtypeStruct((m, n), F32),
        compiler_params=_params(("parallel", "arbitrary")),
        name="matmul",
    )(xn, w)


def _swiglu_body(x_ref, wg_ref, wu_ref, o_ref):
    w = jnp.concatenate([wg_ref[...].astype(BF16), wu_ref[...].astype(BF16)], axis=1)
    gu = jnp.dot(x_ref[...], w, preferred_element_type=F32)
    o_ref[...] = (_silu(gu[:, :FF_TILE]) * gu[:, FF_TILE:]).astype(o_ref.dtype)


def _swiglu_up(xn, w_gu, layer, tm):
    m, k = xn.shape
    nj = D_FF // FF_TILE
    return pl.pallas_call(
        _swiglu_body,
        grid=(m // tm, nj),
        in_specs=[pl.BlockSpec((tm, k), lambda i, j: (i, 0)),
                  pl.BlockSpec((None, k, FF_TILE), lambda i, j: (layer, 0, j)),
                  pl.BlockSpec((None, k, FF_TILE), lambda i, j: (layer, 0, j + nj))],
        out_specs=pl.BlockSpec((tm, FF_TILE), lambda i, j: (i, j)),
        out_shape=jax.ShapeDtypeStruct((m, D_FF), BF16),
        compiler_params=_params(("parallel", "arbitrary")),
        name="swiglu_up",
    )(xn, w_gu, w_gu)


def _down_body(scale, a_ref, w_ref, r_ref, o_ref):
    acc = jnp.dot(a_ref[...], w_ref[...], preferred_element_type=F32)
    o_ref[...] = r_ref[...] + scale * acc


def _down_residual(a, w, layer, res, scale, tm, tn):
    m, k = a.shape
    return pl.pallas_call(
        functools.partial(_down_body, scale),
        grid=(m // tm, D_MODEL // tn),
        in_specs=[pl.BlockSpec((tm, k), lambda i, j: (i, 0), pipeline_mode=pl.Buffered(1)),
                  pl.BlockSpec((None, k, tn), lambda i, j: (layer, 0, j)),
                  pl.BlockSpec((tm, tn), lambda i, j: (i, j))],
        out_specs=pl.BlockSpec((tm, tn), lambda i, j: (i, j)),
        out_shape=jax.ShapeDtypeStruct((m, D_MODEL), F32),
        compiler_params=_params(("parallel", "arbitrary")),
        name="down_residual",
    )(a, w, res)


def _mixout_body(a1_ref, a2_ref, w1_ref, w2_ref, r_ref, o_ref):
    acc = jnp.dot(a1_ref[...].astype(BF16), w1_ref[...].astype(BF16), preferred_element_type=F32)
    acc += jnp.dot(a2_ref[...].astype(BF16), w2_ref[...].astype(BF16), preferred_element_type=F32)
    o_ref[...] = r_ref[...] + acc


def _mix_out_residual(o_hg, o_gdn, w_out, layer, res, tm, tn):
    m = o_hg.shape[0]
    return pl.pallas_call(
        _mixout_body,
        grid=(m // tm, D_MODEL // tn),
        in_specs=[pl.BlockSpec((tm, WIDTH), lambda i, j: (i, 0)),
                  pl.BlockSpec((tm, WIDTH), lambda i, j: (i, 0)),
                  pl.BlockSpec((None, WIDTH, tn), lambda i, j: (layer, 0, j)),
                  pl.BlockSpec((None, WIDTH, tn), lambda i, j: (layer, 1, j)),
                  pl.BlockSpec((tm, tn), lambda i, j: (i, j))],
        out_specs=pl.BlockSpec((tm, tn), lambda i, j: (i, j)),
        out_shape=jax.ShapeDtypeStruct((m, D_MODEL), F32),
        compiler_params=_params(("parallel", "arbitrary")),
        name="mix_out_residual",
    )(o_hg, o_gdn, w_out, w_out, res)


def _lb_body(p_ref, o_ref):
    x = p_ref[...]
    e = jnp.exp(x - jnp.max(x, axis=0, keepdims=True))
    p = e / jnp.sum(e, axis=0, keepdims=True)
    rows = [p[0:1]]
    for l in range(1, DEPTH):
        rows.append(rows[-1] + p[l:l + 1])
    cs = jnp.concatenate(rows, axis=0)
    o_ref[...] = jnp.clip(cs - p[0:1], 0.0, 1.0)


def _lower_bounds(hg_lower_bounds):
    return pl.pallas_call(
        _lb_body,
        out_shape=jax.ShapeDtypeStruct((DEPTH, WIDTH), F32),
        name="lower_bounds",
    )(hg_lower_bounds)


def _hgrn_gates(zq, zf, lb, valid=None):
    q = _silu(zq)
    sig = _sigmoid(zf)
    f = lb + (1.0 - lb) * sig
    logf = jnp.log(jnp.maximum(f, F_FLOOR))
    k = (1.0 - lb) * (1.0 - sig)
    if valid is not None:
        logf = jnp.where(valid, logf, 0.0)
        k = jnp.where(valid, k, 0.0)
    return q, k, logf


def _gdn_wy(qs, ks, vs, d_cols, d_rows, beta_cols):
    c = qs[0].shape[0]
    n = range(len(qs))
    ri = lax.broadcasted_iota(jnp.int32, (c, c), 0)
    ci = lax.broadcasted_iota(jnp.int32, (c, c), 1)
    causal = ri >= ci
    strict = ri > ci
    Ls = [jnp.where(causal, jnp.exp(jnp.where(causal, d_cols[i] - d_rows[i], 0.0)), 0.0) for i in n]
    kks = [_bdot_nt(ks[i], ks[i]) for i in n]
    qks = [_bdot_nt(qs[i], ks[i]) for i in n]
    Ps = [jnp.where(strict, -(beta_cols[i] * kks[i] * Ls[i]), 0.0) for i in n]
    Tms = Ps
    span = 2
    while span < c:
        Ps = [_bdot(Ps[i], Ps[i]) for i in n]
        TPs = [_bdot(Tms[i], Ps[i]) for i in n]
        Tms = [Tms[i] + Ps[i] + TPs[i] for i in n]
        span *= 2
    eds = [jnp.exp(d_cols[i]) for i in n]
    rhss = [jnp.concatenate([vs[i] * beta_cols[i], ks[i] * (beta_cols[i] * eds[i])], axis=1)
            for i in n]
    Xs = [rhss[i] + _bdot(Tms[i], rhss[i]) for i in n]
    out = []
    for i in n:
        d_last = d_cols[i][c - 1:c, :]
        out.append(dict(u=Xs[i][:, :HEAD_DIM], w=Xs[i][:, HEAD_DIM:], qkl=qks[i] * Ls[i],
                        qe=qs[i] * eds[i], kdec=ks[i] * jnp.exp(d_last - d_cols[i]),
                        e_last=jnp.exp(d_last)))
    return out


def _gdn_apply(wy, Ss):
    n = range(len(wy))
    c = wy[0]["u"].shape[0]
    sws = [_bdot(jnp.concatenate([wy[i]["w"], wy[i]["qe"]], axis=0), Ss[i]) for i in n]
    v_news = [wy[i]["u"] - sws[i][:c] for i in n]
    o2s = [_bdot(wy[i]["qkl"], v_news[i]) for i in n]
    upds = [_bdot_tn(wy[i]["kdec"], v_news[i]) for i in n]
    outs = [sws[i][c:] + o2s[i] for i in n]
    return outs, [wy[i]["e_last"] * Ss[i] + upds[i] for i in n]


def _head_norm(o, w):
    return o * lax.rsqrt(jnp.mean(o * o, axis=-1, keepdims=True) + EPS) * w


def _gdn_gate_tile(ab, alog, dtb):
    lane = lax.broadcasted_iota(jnp.int32, ab.shape, 1)
    g = -jnp.exp(alog) * _softplus(ab + dtb)
    return jnp.where(lane < HEADS, g, _sigmoid(ab))


def _l2norm_heads(y, scale):
    outs = []
    for h in range(y.shape[-1] // HEAD_DIM):
        seg = y[:, h * HEAD_DIM:(h + 1) * HEAD_DIM]
        outs.append(seg * (lax.rsqrt(jnp.sum(seg * seg, axis=-1, keepdims=True) + EPS) * scale))
    return jnp.concatenate(outs, axis=-1)


H_TB = 128
G_TB = 128
G_HG = 8


def _causal_conv_silu(x_ref, w_ref, carry_ref):
    x = x_ref[...]
    tb = x.shape[0]
    carry = carry_ref[...]
    row8 = lax.broadcasted_iota(jnp.int32, carry.shape, 0)
    y = x * w_ref[CONV_W - 1:CONV_W, :]
    for j in range(1, CONV_W):
        xr = pltpu.roll(x, j, axis=0)
        top = jnp.where(row8 < j, pltpu.roll(carry, j, axis=0), xr[:SUBLANES])
        sh = jnp.concatenate([top, xr[SUBLANES:]], axis=0)
        y = y + sh * w_ref[CONV_W - 1 - j:CONV_W - j, :]
    carry_ref[...] = x[tb - SUBLANES:]
    return _silu(y)


def _hgrn_prompt_chunk(zq, zf, v, gate, lb, nw, s_scr, k_scr, b_scr):
    heads = range(HEADS)
    hc = [slice(h * HEAD_DIM, (h + 1) * HEAD_DIM) for h in heads]
    band = SUBLANES
    q, k, logf = _hgrn_gates(zq, zf, lb)
    b = _cumsum_rows(logf) * LOG2E
    k_scr[...] = k
    b_scr[...] = b
    bl = b[CHUNK - 1:CHUNK, :]
    qe = q * jnp.exp2(b)
    kdec = k * jnp.exp2(bl - b)
    e_last = jnp.exp2(bl)

    lane = lax.broadcasted_iota(jnp.int32, (band, CHUNK), 1)
    row = lax.broadcasted_iota(jnp.int32, (band, CHUNK), 0)
    a_bands = [[None] * (CHUNK // band) for _ in heads]
    for i in range(CHUNK // SUB):
        base = i * SUB
        q_lo, q_hi = q[base:base + band], q[base + band:base + SUB]
        b_lo, b_hi = b[base:base + band], b[base + band:base + SUB]
        if i == 0:
            a_lo = [jnp.zeros((band, CHUNK), F32) for _ in heads]
            a_hi = [jnp.zeros((band, CHUNK), F32) for _ in heads]
        else:
            ref = b[base - 1:base, :]
            qt = q[base:base + SUB] * jnp.exp2(b[base:base + SUB] - ref)
            kt = k * jnp.exp2(jnp.minimum(ref - b, 0.0))
            att = [jnp.where(lane[:1] < base, _bdot_nt(qt[:, hc[h]], kt[:, hc[h]]), 0.0)
                   for h in heads]
            a_lo = [att[h][:band] for h in heads]
            a_hi = [att[h][band:] for h in heads]
        for s in range(SUB):
            r = base + s
            k_s = k_scr[r:r + 1, :]
            b_s = b_scr[r:r + 1, :]
            p_hi = q_hi * k_s * jnp.exp2(jnp.minimum(b_hi - b_s, 0.0))
            if s < band:
                p_lo = q_lo * k_s * jnp.exp2(jnp.minimum(b_lo - b_s, 0.0))
                m_lo = (lane == r) & (row >= s)
                m_hi = lane == r
            else:
                m_hi = (lane == r) & (row >= s - band)
            for h in heads:
                if s < band:
                    a_lo[h] = jnp.where(m_lo, jnp.sum(p_lo[:, hc[h]], axis=-1, keepdims=True),
                                        a_lo[h])
                a_hi[h] = jnp.where(m_hi, jnp.sum(p_hi[:, hc[h]], axis=-1, keepdims=True),
                                    a_hi[h])
        for h in heads:
            a_bands[h][2 * i] = a_lo[h]
            a_bands[h][2 * i + 1] = a_hi[h]

    states = [s_scr[h] for h in heads]
    o_inter = [_bdot(qe[:, hc[h]], states[h]) for h in heads]
    o_intra = [_bdot(jnp.concatenate(a_bands[h], axis=0), v[:, hc[h]]) for h in heads]
    upd = [_bdot_tn(kdec[:, hc[h]], v[:, hc[h]]) for h in heads]
    outs = []
    for h in heads:
        s_scr[h] = _row_to_col(e_last[:, hc[h]]) * states[h] + upd[h]
        outs.append(_head_norm(o_inter[h] + o_intra[h], nw))
    return jnp.concatenate(outs, axis=1) * _sigmoid(gate)


def _hgrn_prompt_body(q_ref, f_ref, i_ref, gate_ref, lb_ref, nw_ref, o_ref, s_ref,
                      s_scr, k_scr, b_scr):
    t = pl.program_id(1)

    @pl.when(t == 0)
    def _():
        s_scr[...] = jnp.zeros_like(s_scr)

    lb = lb_ref[...]
    nw = nw_ref[...]
    for c in range(H_TB // CHUNK):
        rows = slice(c * CHUNK, (c + 1) * CHUNK)
        o = _hgrn_prompt_chunk(q_ref[rows, :], f_ref[rows, :], i_ref[rows, :], gate_ref[rows, :],
                               lb, nw, s_scr, k_scr, b_scr)
        o_ref[rows, :] = o.astype(o_ref.dtype)

    @pl.when(t == pl.num_programs(1) - 1)
    def _():
        s_ref[0] = s_scr[...]


def _hgrn_prompt(proj, lb, hg_norm, layer, batch, seq):
    nt = seq // H_TB
    blk = lambda r: pl.BlockSpec((H_TB, WIDTH), lambda b, t: (b * nt + t, r))
    return pl.pallas_call(
        _hgrn_prompt_body,
        grid=(batch, nt),
        in_specs=[blk(0), blk(1), blk(2), blk(3),
                  pl.BlockSpec((None, 1, WIDTH), lambda b, t: (layer, 0, 0)),
                  pl.BlockSpec((None, 1, HEAD_DIM), lambda b, t: (layer, 0, 0))],
        out_specs=[pl.BlockSpec((H_TB, WIDTH), lambda b, t: (b * nt + t, 0)),
                   pl.BlockSpec((1, HEADS, HEAD_DIM, HEAD_DIM), lambda b, t: (b, 0, 0, 0))],
        out_shape=[jax.ShapeDtypeStruct((batch * seq, WIDTH), BF16),
                   jax.ShapeDtypeStruct((batch, HEADS, HEAD_DIM, HEAD_DIM), F32)],
        scratch_shapes=[pltpu.VMEM((HEADS, HEAD_DIM, HEAD_DIM), F32),
                        pltpu.VMEM((CHUNK, WIDTH), F32),
                        pltpu.VMEM((CHUNK, WIDTH), F32)],
        compiler_params=_params(("parallel", "arbitrary")),
        name="hgrn_prompt",
    )(proj, proj, proj, proj, lb, hg_norm)


def _gdn_prompt_body(xq_ref, xk_ref, xv_ref, wq_ref, wk_ref, wv_ref, z_ref, ab_ref, alog_ref,
                     dtb_ref, nw_ref, o_ref, s_ref, s_scr, q_scr, k_scr, v_scr, cq, ck, cv):
    t = pl.program_id(1)

    @pl.when(t == 0)
    def _():
        s_scr[...] = jnp.zeros_like(s_scr)
        cq[...] = jnp.zeros_like(cq)
        ck[...] = jnp.zeros_like(ck)
        cv[...] = jnp.zeros_like(cv)

    q_scr[...] = _l2norm_heads(_causal_conv_silu(xq_ref, wq_ref, cq), HEAD_DIM ** -0.5)
    k_scr[...] = _l2norm_heads(_causal_conv_silu(xk_ref, wk_ref, ck), 1.0)
    v_scr[...] = _causal_conv_silu(xv_ref, wv_ref, cv)

    nw = nw_ref[...]
    gb = _gdn_gate_tile(ab_ref[...], alog_ref[...], dtb_ref[...])
    nc = G_TB // CHUNK
    d_all = jnp.concatenate(
        [_cumsum_rows(gb[c * CHUNK:(c + 1) * CHUNK]) for c in range(nc)], axis=0)
    d_t = d_all.T
    for c in range(nc):
        rows = slice(c * CHUNK, (c + 1) * CHUNK)
        for h0 in range(0, HEADS, G_HG):
            heads = list(range(h0, h0 + G_HG))
            cols = [slice(h * HEAD_DIM, (h + 1) * HEAD_DIM) for h in heads]
            wy = _gdn_wy([q_scr[rows, cs] for cs in cols],
                         [k_scr[rows, cs] for cs in cols],
                         [v_scr[rows, cs] for cs in cols],
                         [d_all[rows, h:h + 1] for h in heads],
                         [d_t[h:h + 1, rows] for h in heads],
                         [gb[rows, HEADS + h:HEADS + h + 1] for h in heads])
            outs, new_states = _gdn_apply(wy, [s_scr[h] for h in heads])
            for i, h in enumerate(heads):
                s_scr[h] = new_states[i]
                o = _head_norm(outs[i], nw) * _silu(z_ref[rows, cols[i]])
                o_ref[rows, cols[i]] = o.astype(o_ref.dtype)

    @pl.when(t == pl.num_programs(1) - 1)
    def _():
        s_ref[0] = s_scr[...]


def _gdn_prompt(proj, conv_w, z, ab, alog, dtb, gdn_norm, layer, batch, seq):
    nt = seq // G_TB
    qkv_blk = 4 * WIDTH // WIDTH
    blk = lambda r: pl.BlockSpec((G_TB, WIDTH), lambda b, t: (b * nt + t, r))
    cw = lambda c: pl.BlockSpec((None, CONV_W, WIDTH), lambda b, t: (layer, 0, c))
    row = lambda n: pl.BlockSpec((None, 1, n), lambda b, t: (layer, 0, 0))
    return pl.pallas_call(
        _gdn_prompt_body,
        grid=(batch, nt),
        in_specs=[blk(qkv_blk), blk(qkv_blk + 1), blk(qkv_blk + 2), cw(0), cw(1), cw(2), blk(0),
                  pl.BlockSpec((G_TB, LANES), lambda b, t: (b * nt + t, 0)),
                  row(LANES), row(LANES), row(HEAD_DIM)],
        out_specs=[pl.BlockSpec((G_TB, WIDTH), lambda b, t: (b * nt + t, 0)),
                   pl.BlockSpec((1, HEADS, HEAD_DIM, HEAD_DIM), lambda b, t: (b, 0, 0, 0))],
        out_shape=[jax.ShapeDtypeStruct((batch * seq, WIDTH), BF16),
                   jax.ShapeDtypeStruct((batch, HEADS, HEAD_DIM, HEAD_DIM), F32)],
        scratch_shapes=[pltpu.VMEM((HEADS, HEAD_DIM, HEAD_DIM), F32)]
        + [pltpu.VMEM((G_TB, WIDTH), F32)] * 3 + [pltpu.VMEM((SUBLANES, WIDTH), F32)] * 3,
        compiler_params=_params(("parallel", "arbitrary")),
        name="gdn_prompt",
    )(proj, proj, proj, conv_w, conv_w, conv_w, z, ab, alog, dtb, gdn_norm)


S_BB = 8
S_T = 4
S_C = 8


def _conv_sample_body(x_ref, buf_ref, w_ref, o_ref):
    c = pl.program_id(1)
    xx = [buf_ref[:, j, :] for j in range(CONV_W - 1)] + [x_ref[:, t, :] for t in range(S_T)]
    for t in range(S_T):
        y = xx[t] * w_ref[0:1, :]
        for j in range(1, CONV_W):
            y = y + xx[t + j] * w_ref[j:j + 1, :]
        y = _silu(y)
        scale = jnp.where(c == 0, HEAD_DIM ** -0.5, 1.0)
        yn = _l2norm_heads(y, scale)
        o_ref[:, t, :] = jnp.where(c == 2, y, yn)


def _conv_sample(proj3, buf, conv_w, layer):
    nb = proj3.shape[0]
    qkv_blk = 4
    return pl.pallas_call(
        _conv_sample_body,
        grid=(nb // S_BB, 3),
        in_specs=[pl.BlockSpec((S_BB, S_T, WIDTH), lambda i, c: (i, 0, qkv_blk + c)),
                  pl.BlockSpec((None, S_BB, CONV_W - 1, WIDTH), lambda i, c: (layer, i, 0, c)),
                  pl.BlockSpec((None, CONV_W, WIDTH), lambda i, c: (layer, 0, c))],
        out_specs=pl.BlockSpec((S_BB, S_T, WIDTH), lambda i, c: (i, 0, c)),
        out_shape=jax.ShapeDtypeStruct((nb, S_T, CONV_DIM), F32),
        compiler_params=_params(("parallel", "arbitrary")),
        name="conv_sample",
    )(proj3, buf, conv_w)


def _load_padded(dst_ref, src_ref):
    dst_ref[...] = jnp.zeros_like(dst_ref)
    dst_ref[0:S_T, :] = src_ref[0]


def _hgrn_sample_body(q_ref, f_ref, i_ref, gate_ref, lb_ref, nw_ref, s0_ref,
                      o_ref, s_ref, qp, fp, ip, gp, op):
    _load_padded(qp, q_ref)
    _load_padded(fp, f_ref)
    _load_padded(ip, i_ref)
    _load_padded(gp, gate_ref)
    nw = nw_ref[...]
    heads = range(HEADS)
    cols = [slice(h * HEAD_DIM, (h + 1) * HEAD_DIM) for h in heads]
    valid = lax.broadcasted_iota(jnp.int32, (S_C, WIDTH), 0) < S_T
    row = lax.broadcasted_iota(jnp.int32, (S_C, 1), 0)
    q, k, logf = _hgrn_gates(qp[...], fp[...], lb_ref[...], valid)
    v = ip[...]
    b = _cumsum_rows(logf)
    bl = b[S_T - 1:S_T, :]
    qe = q * jnp.exp(b)
    kdec = k * jnp.exp(bl - b)
    e_last = jnp.exp(bl)
    o_diag = [jnp.zeros((S_C, HEAD_DIM), F32) for _ in heads]
    for s in range(S_T):
        p = q * k[s:s + 1, :] * jnp.exp(jnp.minimum(b - b[s:s + 1, :], 0.0))
        col = [jnp.where(row >= s, jnp.sum(p[:, cols[h]], axis=-1, keepdims=True), 0.0)
               for h in heads]
        o_diag = [o_diag[h] + col[h] * v[s:s + 1, cols[h]] for h in heads]
    s0 = [s0_ref[0, h] for h in heads]
    o_inter = [_bdot(qe[:, cols[h]], s0[h]) for h in heads]
    upd = [_bdot_tn(kdec[:, cols[h]], v[:, cols[h]]) for h in heads]
    for h in heads:
        s_ref[0, h] = _row_to_col(e_last[:, cols[h]]) * s0[h] + upd[h]
        op[:, cols[h]] = _head_norm(o_inter[h] + o_diag[h], nw)
    o_ref[0] = (op[...] * _sigmoid(gp[...]))[0:S_T, :]


def _hgrn_sample(proj3, lb, hg_norm, s0, layer):
    nb = proj3.shape[0]
    blk = lambda r: pl.BlockSpec((1, S_T, WIDTH), lambda b: (b, 0, r))
    st = pl.BlockSpec((None, 1, HEADS, HEAD_DIM, HEAD_DIM), lambda b: (layer, b, 0, 0, 0))
    return pl.pallas_call(
        _hgrn_sample_body,
        grid=(nb,),
        in_specs=[blk(0), blk(1), blk(2), blk(3),
                  pl.BlockSpec((None, 1, WIDTH), lambda b: (layer, 0, 0)),
                  pl.BlockSpec((None, 1, HEAD_DIM), lambda b: (layer, 0, 0)),
                  st],
        out_specs=[pl.BlockSpec((1, S_T, WIDTH), lambda b: (b, 0, 0)),
                   pl.BlockSpec((1, HEADS, HEAD_DIM, HEAD_DIM), lambda b: (b, 0, 0, 0))],
        out_shape=[jax.ShapeDtypeStruct((nb, S_T, WIDTH), F32),
                   jax.ShapeDtypeStruct((nb, HEADS, HEAD_DIM, HEAD_DIM), F32)],
        scratch_shapes=[pltpu.VMEM((S_C, WIDTH), F32)] * 5,
        compiler_params=_params(("parallel",)),
        name="hgrn_sample",
    )(proj3, proj3, proj3, proj3, lb, hg_norm, s0)


def _gdn_sample_body(q_ref, k_ref, v_ref, z_ref, ab_ref, alog_ref, dtb_ref, nw_ref, s0_ref,
                     o_ref, s_ref, qp, kp, vp, zp, abp, op):
    _load_padded(qp, q_ref)
    _load_padded(kp, k_ref)
    _load_padded(vp, v_ref)
    _load_padded(zp, z_ref)
    _load_padded(abp, ab_ref)
    nw = nw_ref[...]
    valid = lax.broadcasted_iota(jnp.int32, (S_C, LANES), 0) < S_T
    gb = jnp.where(valid, _gdn_gate_tile(abp[...], alog_ref[...], dtb_ref[...]), 0.0)
    d_all = _cumsum_rows(gb)
    row = lax.broadcasted_iota(jnp.int32, (S_C, 1), 0)
    heads = range(HEADS)
    cols = [slice(h * HEAD_DIM, (h + 1) * HEAD_DIM) for h in heads]
    qw, kw = qp[...], kp[...]
    qs = [qw[:, c] for c in cols]
    ks = [kw[:, c] for c in cols]
    ds = [d_all[:, h:h + 1] for h in heads]
    betas = [gb[:, HEADS + h:HEADS + h + 1] for h in heads]
    eds = [jnp.exp(d) for d in ds]
    Xs = [jnp.concatenate([vp[:, cols[h]] * betas[h], ks[h] * (betas[h] * eds[h])], axis=1)
          for h in heads]
    qkl = [[None] * S_T for _ in heads]
    for s in range(S_T):
        qk_w = qw * kw[s:s + 1, :]
        kk_w = kw * kw[s:s + 1, :]
        decay = [jnp.where(row >= s, jnp.exp(jnp.where(row >= s, ds[h] - ds[h][s:s + 1, :], 0.0)),
                           0.0) for h in heads]
        qk_s = [jnp.sum(qk_w[:, cols[h]], axis=-1, keepdims=True) for h in heads]
        for h in heads:
            qkl[h][s] = qk_s[h] * decay[h]
        if s < S_T - 1:
            kk_s = [jnp.sum(kk_w[:, cols[h]], axis=-1, keepdims=True) for h in heads]
            m_s = [jnp.where(row > s, betas[h] * kk_s[h] * decay[h], 0.0) for h in heads]
            Xs = [Xs[h] - m_s[h] * Xs[h][s:s + 1, :] for h in heads]
    d_last = [d[S_T - 1:S_T, :] for d in ds]
    qes = [qs[h] * eds[h] for h in heads]
    kdecs = [ks[h] * jnp.exp(d_last[h] - ds[h]) for h in heads]
    s0 = [s0_ref[0, h] for h in heads]
    sws = [_bdot(jnp.concatenate([Xs[h][:, HEAD_DIM:], qes[h]], axis=0), s0[h]) for h in heads]
    v_news = [Xs[h][:, :HEAD_DIM] - sws[h][:S_C] for h in heads]
    upds = [_bdot_tn(kdecs[h], v_news[h]) for h in heads]
    for h in heads:
        o = sws[h][S_C:]
        for s in range(S_T):
            o = o + qkl[h][s] * v_news[h][s:s + 1, :]
        s_ref[0, h] = jnp.exp(d_last[h]) * s0[h] + upds[h]
        op[:, cols[h]] = _head_norm(o, nw) * _silu(zp[:, cols[h]])
    o_ref[0] = op[0:S_T, :]


def _gdn_sample(qkv3, z3, ab3, alog, dtb, gdn_norm, s0, layer):
    nb = qkv3.shape[0]
    blk = lambda r: pl.BlockSpec((1, S_T, WIDTH), lambda b: (b, 0, r))
    return pl.pallas_call(
        _gdn_sample_body,
        grid=(nb,),
        in_specs=[blk(0), blk(1), blk(2), blk(0),
                  pl.BlockSpec((1, S_T, LANES), lambda b: (b, 0, 0)),
                  pl.BlockSpec((None, 1, LANES), lambda b: (layer, 0, 0)),
                  pl.BlockSpec((None, 1, LANES), lambda b: (layer, 0, 0)),
                  pl.BlockSpec((None, 1, HEAD_DIM), lambda b: (layer, 0, 0)),
                  pl.BlockSpec((None, 1, HEADS, HEAD_DIM, HEAD_DIM), lambda b: (layer, b, 0, 0, 0))],
        out_specs=[pl.BlockSpec((1, S_T, WIDTH), lambda b: (b, 0, 0)),
                   pl.BlockSpec((1, HEADS, HEAD_DIM, HEAD_DIM), lambda b: (b, 0, 0, 0))],
        out_shape=[jax.ShapeDtypeStruct((nb, S_T, WIDTH), F32),
                   jax.ShapeDtypeStruct((nb, HEADS, HEAD_DIM, HEAD_DIM), F32)],
        scratch_shapes=[pltpu.VMEM((S_C, WIDTH), F32)] * 4 + [pltpu.VMEM((S_C, LANES), F32),
                                                              pltpu.VMEM((S_C, WIDTH), F32)],
        compiler_params=_params(("parallel",)),
        name="gdn_sample",
    )(qkv3, qkv3, qkv3, z3, ab3, alog, dtb, gdn_norm, s0)


def _trunk(x, sample, states, W):
    rows = x.shape[0]
    tm = 512 if sample else 1024
    hg_states, gdn_states, conv_states = [], [], []
    for l in range(DEPTH):
        xn = _rmsnorm(x, W["ffn1_norm"][l], BF16)
        a = _swiglu_up(xn, W["ffn1_w_gu"], l, tm)
        x = _down_residual(a, W["ffn1_w_down"], l, x, 0.5, tm, 256)

        xn = _rmsnorm(x, W["mix_norm"][l], BF16)
        proj = _matmul(xn, W["w_in"], l, tm, 1024, MAIN_COLS)
        z = _matmul(xn, W["w_in_z"], l, tm, 1024, WIDTH)
        ab = _matmul(xn, W["w_in_ab"], l, tm, LANES, LANES)
        if sample:
            nb = rows // S_T
            proj3 = proj.reshape(nb, S_T, MAIN_COLS)
            ab3 = ab.reshape(nb, S_T, LANES)
            qkv3 = _conv_sample(proj3, states["conv"], W["conv_w"], l)
            o_hg, s_hg = _hgrn_sample(proj3, W["lb"], W["hg_norm"], states["hgrn"], l)
            o_gdn, s_gdn = _gdn_sample(qkv3, z.reshape(nb, S_T, WIDTH), ab3, W["alog"], W["dtb"],
                                       W["gdn_norm"], states["gdn"], l)
            o_hg = o_hg.reshape(rows, WIDTH)
            o_gdn = o_gdn.reshape(rows, WIDTH)
            conv_new = proj3[:, S_T - (CONV_W - 1):, 4 * WIDTH:4 * WIDTH + CONV_DIM]
        else:
            batch, seq = states["batch"], states["seq"]
            o_hg, s_hg = _hgrn_prompt(proj, W["lb"], W["hg_norm"], l, batch, seq)
            o_gdn, s_gdn = _gdn_prompt(proj, W["conv_w"], z, ab, W["alog"], W["dtb"],
                                       W["gdn_norm"], l, batch, seq)
            conv_new = proj.reshape(batch, seq, MAIN_COLS)[
                :, seq - (CONV_W - 1):, 4 * WIDTH:4 * WIDTH + CONV_DIM]
        x = _mix_out_residual(o_hg, o_gdn, W["w_out"], l, x, tm, 512)

        xn = _rmsnorm(x, W["ffn2_norm"][l], BF16)
        a = _swiglu_up(xn, W["ffn2_w_gu"], l, tm)
        x = _down_residual(a, W["ffn2_w_down"], l, x, 0.5, tm, 256)

        hg_states.append(s_hg)
        gdn_states.append(s_gdn)
        conv_states.append(conv_new)
    y = _rmsnorm(x, W["final_norm"], F32)
    return y, jnp.stack(hg_states), jnp.stack(gdn_states), jnp.stack(conv_states)


def kernel(x_prompt, x_sample, state_hgrn, state_gdn, state_conv, ffn1_norm, ffn1_w_gu, ffn1_w_down, mix_norm, w_in, hg_lower_bounds, hg_norm, gdn_conv_w, gdn_A_log, gdn_dt_bias, gdn_norm, w_out, ffn2_norm, ffn2_w_gu, ffn2_w_down, final_norm):
    batch, seq, _ = x_prompt.shape
    dec_batch, dec_seq, _ = x_sample.shape
    assert dec_seq == S_T and seq % H_TB == 0 and seq % G_TB == 0

    ab_lo = 4 * WIDTH + CONV_DIM
    row3 = lambda v: v.astype(F32)[:, None, :]
    pad_row = lambda v: row3(jnp.pad(v, ((0, 0), (0, LANES - v.shape[1]))))
    W = {
        "ffn1_norm": ffn1_norm, "mix_norm": mix_norm, "ffn2_norm": ffn2_norm,
        "final_norm": final_norm, "hg_norm": row3(hg_norm), "gdn_norm": row3(gdn_norm),
        "conv_w": gdn_conv_w,
        "ffn1_w_gu": ffn1_w_gu, "ffn1_w_down": ffn1_w_down.astype(BF16),
        "ffn2_w_gu": ffn2_w_gu, "ffn2_w_down": ffn2_w_down.astype(BF16),
        "w_out": w_out,
        "w_in": w_in[:, :, :ab_lo].astype(BF16),
        "w_in_z": w_in[:, :, ab_lo + 2 * HEADS:].astype(BF16),
        "w_in_ab": jnp.pad(w_in[:, :, ab_lo:ab_lo + 2 * HEADS],
                           ((0, 0), (0, 0), (0, LANES - 2 * HEADS))).astype(BF16),
        "alog": pad_row(gdn_A_log), "dtb": pad_row(gdn_dt_bias),
        "lb": row3(_lower_bounds(hg_lower_bounds)),
    }

    y_p, hg_p, gdn_p, conv_p = _trunk(
        x_prompt.reshape(batch * seq, D_MODEL), False, {"batch": batch, "seq": seq}, W)
    y_s, hg_s, gdn_s, conv_s = _trunk(
        x_sample.reshape(dec_batch * dec_seq, D_MODEL), True,
        {"hgrn": state_hgrn, "gdn": state_gdn, "conv": state_conv}, W)
    return (y_p.reshape(batch, seq, D_MODEL), y_s.reshape(dec_batch, dec_seq, D_MODEL),
            hg_p, gdn_p, conv_p, hg_s, gdn_s, conv_s)
```

```python
import functools

import jax
import jax.numpy as jnp
from jax import lax
from jax.experimental import pallas as pl
from jax.experimental.pallas import tpu as pltpu

D_MODEL = 4096
DEPTH = 4
HEAD_DIM = 128
HEADS = 16
WIDTH = HEADS * HEAD_DIM
CONV_W = 4
CONV_DIM = 3 * WIDTH
D_FF = 11008
EPS = 1e-6
F_FLOOR = 1e-20
LOG2E = 1.4426950408889634
LANES = 128
SUBLANES = 8
FF_TILE = 256
CHUNK = 64
SUB = 16
MAIN_COLS = 4 * WIDTH + CONV_DIM
VMEM_LIMIT = 56 * 1024 * 1024

BF16 = jnp.bfloat16
F32 = jnp.float32


def _params(sem, vmem=VMEM_LIMIT):
    return pltpu.CompilerParams(dimension_semantics=sem, vmem_limit_bytes=vmem)


def _bdot(a, b):
    return jnp.dot(a.astype(BF16), b.astype(BF16), preferred_element_type=F32)


def _bdot_nt(a, b):
    return lax.dot_general(a.astype(BF16), b.astype(BF16), (((1,), (1,)), ((), ())),
                           preferred_element_type=F32)


def _bdot_tn(a, b):
    return lax.dot_general(a.astype(BF16), b.astype(BF16), (((0,), (0,)), ((), ())),
                           preferred_element_type=F32)


def _sigmoid(x):
    return 1.0 / (1.0 + jnp.exp(-x))


def _silu(x):
    return x * _sigmoid(x)


def _softplus(x):
    return jnp.maximum(x, 0.0) + jnp.log(1.0 + jnp.exp(-jnp.abs(x)))


def _cumsum_rows(x):
    n = x.shape[0]
    row = lax.broadcasted_iota(jnp.int32, x.shape, 0)
    s = 1
    while s < n:
        x = x + jnp.where(row >= s, pltpu.roll(x, s, axis=0), 0.0)
        s *= 2
    return x


def _row_to_col(r):
    n = r.shape[1]
    ri = lax.broadcasted_iota(jnp.int32, (n, n), 0)
    ci = lax.broadcasted_iota(jnp.int32, (n, n), 1)
    return jnp.sum(jnp.where(ri == ci, jnp.broadcast_to(r, (n, n)), 0.0), axis=1, keepdims=True)


def _col_to_row(c):
    n = c.shape[0]
    ri = lax.broadcasted_iota(jnp.int32, (n, n), 0)
    ci = lax.broadcasted_iota(jnp.int32, (n, n), 1)
    return jnp.sum(jnp.where(ri == ci, jnp.broadcast_to(c, (n, n)), 0.0), axis=0, keepdims=True)


def _rmsnorm_body(x_ref, w_ref, o_ref):
    x = x_ref[...]
    y = x * lax.rsqrt(jnp.mean(x * x, axis=-1, keepdims=True) + EPS)
    o_ref[...] = (y * w_ref[...]).astype(o_ref.dtype)


def _rmsnorm(x, w, out_dtype, tm=512):
    m = x.shape[0]
    return pl.pallas_call(
        _rmsnorm_body,
        grid=(m // tm,),
        in_specs=[pl.BlockSpec((tm, D_MODEL), lambda i: (i, 0)),
                  pl.BlockSpec((1, D_MODEL), lambda i: (0, 0))],
        out_specs=pl.BlockSpec((tm, D_MODEL), lambda i: (i, 0)),
        out_shape=jax.ShapeDtypeStruct((m, D_MODEL), out_dtype),
        compiler_params=_params(("parallel",)),
        name="rmsnorm",
    )(x, w.reshape(1, D_MODEL))


def _mm_body(x_ref, w_ref, o_ref):
    o_ref[...] = jnp.dot(x_ref[...], w_ref[...].astype(BF16), preferred_element_type=F32)


def _matmul(xn, w, layer, tm, tn, n):
    m, k = xn.shape
    return pl.pallas_call(
        _mm_body,
        grid=(m // tm, n // tn),
        in_specs=[pl.BlockSpec((tm, k), lambda i, j: (i, 0)),
                  pl.BlockSpec((None, k, tn), lambda i, j: (layer, 0, j))],
        out_specs=pl.BlockSpec((tm, tn), lambda i, j: (i, j)),
        out_shape=jax.ShapeDtypeStruct((m, n), F32),
        compiler_params=_params(("parallel", "arbitrary")),
        name="matmul",
    )(xn, w)


def _swiglu_body(x_ref, wg_ref, wu_ref, o_ref):
    w = jnp.concatenate([wg_ref[...].astype(BF16), wu_ref[...].astype(BF16)], axis=1)
    gu = jnp.dot(x_ref[...], w, preferred_element_type=F32)
    o_ref[...] = (_silu(gu[:, :FF_TILE]) * gu[:, FF_TILE:]).astype(o_ref.dtype)


def _swiglu_up(xn, w_gu, layer, tm):
    m, k = xn.shape
    nj = D_FF // FF_TILE
    return pl.pallas_call(
        _swiglu_body,
        grid=(m // tm, nj),
        in_specs=[pl.BlockSpec((tm, k), lambda i, j: (i, 0), pipeline_mode=pl.Buffered(1)),
                  pl.BlockSpec((None, k, FF_TILE), lambda i, j: (layer, 0, j)),
                  pl.BlockSpec((None, k, FF_TILE), lambda i, j: (layer, 0, j + nj))],
        out_specs=pl.BlockSpec((tm, FF_TILE), lambda i, j: (i, j)),
        out_shape=jax.ShapeDtypeStruct((m, D_FF), BF16),
        compiler_params=_params(("parallel", "arbitrary")),
        name="swiglu_up",
    )(xn, w_gu, w_gu)


def _down_body(scale, a_ref, w_ref, r_ref, o_ref):
    acc = jnp.dot(a_ref[...], w_ref[...], preferred_element_type=F32)
    o_ref[...] = r_ref[...] + scale * acc


def _down_residual(a, w, layer, res, scale, tm, tn):
    m, k = a.shape
    return pl.pallas_call(
        functools.partial(_down_body, scale),
        grid=(m // tm, D_MODEL // tn),
        in_specs=[pl.BlockSpec((tm, k), lambda i, j: (i, 0)),
                  pl.BlockSpec((None, k, tn), lambda i, j: (layer, 0, j)),
                  pl.BlockSpec((tm, tn), lambda i, j: (i, j))],
        out_specs=pl.BlockSpec((tm, tn), lambda i, j: (i, j)),
        out_shape=jax.ShapeDtypeStruct((m, D_MODEL), F32),
        compiler_params=_params(("parallel", "arbitrary")),
        name="down_residual",
    )(a, w, res)


def _mixout_body(a1_ref, a2_ref, w1_ref, w2_ref, r_ref, o_ref):
    acc = jnp.dot(a1_ref[...].astype(BF16), w1_ref[...].astype(BF16), preferred_element_type=F32)
    acc += jnp.dot(a2_ref[...].astype(BF16), w2_ref[...].astype(BF16), preferred_element_type=F32)
    o_ref[...] = r_ref[...] + acc


def _mix_out_residual(o_hg, o_gdn, w_out, layer, res, tm, tn):
    m = o_hg.shape[0]
    return pl.pallas_call(
        _mixout_body,
        grid=(m // tm, D_MODEL // tn),
        in_specs=[pl.BlockSpec((tm, WIDTH), lambda i, j: (i, 0)),
                  pl.BlockSpec((tm, WIDTH), lambda i, j: (i, 0)),
                  pl.BlockSpec((None, WIDTH, tn), lambda i, j: (layer, 0, j)),
                  pl.BlockSpec((None, WIDTH, tn), lambda i, j: (layer, 1, j)),
                  pl.BlockSpec((tm, tn), lambda i, j: (i, j))],
        out_specs=pl.BlockSpec((tm, tn), lambda i, j: (i, j)),
        out_shape=jax.ShapeDtypeStruct((m, D_MODEL), F32),
        compiler_params=_params(("parallel", "arbitrary")),
        name="mix_out_residual",
    )(o_hg, o_gdn, w_out, w_out, res)


def _lb_body(p_ref, o_ref):
    x = p_ref[...]
    e = jnp.exp(x - jnp.max(x, axis=0, keepdims=True))
    p = e / jnp.sum(e, axis=0, keepdims=True)
    rows = [p[0:1]]
    for l in range(1, DEPTH):
        rows.append(rows[-1] + p[l:l + 1])
    cs = jnp.concatenate(rows, axis=0)
    o_ref[...] = jnp.clip(cs - p[0:1], 0.0, 1.0)


def _lower_bounds(hg_lower_bounds):
    return pl.pallas_call(
        _lb_body,
        out_shape=jax.ShapeDtypeStruct((DEPTH, WIDTH), F32),
        name="lower_bounds",
    )(hg_lower_bounds)


def _hgrn_gates(zq, zf, lb, valid=None):
    q = _silu(zq)
    sig = _sigmoid(zf)
    f = lb + (1.0 - lb) * sig
    logf = jnp.log(jnp.maximum(f, F_FLOOR))
    k = (1.0 - lb) * (1.0 - sig)
    if valid is not None:
        logf = jnp.where(valid, logf, 0.0)
        k = jnp.where(valid, k, 0.0)
    return q, k, logf


def _gdn_wy(qs, ks, vs, d_cols, d_rows, beta_cols):
    c = qs[0].shape[0]
    n = range(len(qs))
    ri = lax.broadcasted_iota(jnp.int32, (c, c), 0)
    ci = lax.broadcasted_iota(jnp.int32, (c, c), 1)
    causal = ri >= ci
    strict = ri > ci
    Ls = [jnp.where(causal, jnp.exp(jnp.where(causal, d_cols[i] - d_rows[i], 0.0)), 0.0) for i in n]
    kks = [_bdot_nt(ks[i], ks[i]) for i in n]
    qks = [_bdot_nt(qs[i], ks[i]) for i in n]
    Ps = [jnp.where(strict, -(beta_cols[i] * kks[i] * Ls[i]), 0.0) for i in n]
    Tms = Ps
    span = 2
    while span < c:
        Ps = [_bdot(Ps[i], Ps[i]) for i in n]
        TPs = [_bdot(Tms[i], Ps[i]) for i in n]
        Tms = [Tms[i] + Ps[i] + TPs[i] for i in n]
        span *= 2
    eds = [jnp.exp(d_cols[i]) for i in n]
    rhss = [jnp.concatenate([vs[i] * beta_cols[i], ks[i] * (beta_cols[i] * eds[i])], axis=1)
            for i in n]
    Xs = [rhss[i] + _bdot(Tms[i], rhss[i]) for i in n]
    out = []
    for i in n:
        d_last = d_cols[i][c - 1:c, :]
        out.append(dict(u=Xs[i][:, :HEAD_DIM], w=Xs[i][:, HEAD_DIM:], qkl=qks[i] * Ls[i],
                        qe=qs[i] * eds[i], kdec=ks[i] * jnp.exp(d_last - d_cols[i]),
                        e_last=jnp.exp(d_last)))
    return out


def _gdn_apply(wy, Ss):
    n = range(len(wy))
    c = wy[0]["u"].shape[0]
    sws = [_bdot(jnp.concatenate([wy[i]["w"], wy[i]["qe"]], axis=0), Ss[i]) for i in n]
    v_news = [wy[i]["u"] - sws[i][:c] for i in n]
    o2s = [_bdot(wy[i]["qkl"], v_news[i]) for i in n]
    upds = [_bdot_tn(wy[i]["kdec"], v_news[i]) for i in n]
    outs = [sws[i][c:] + o2s[i] for i in n]
    return outs, [wy[i]["e_last"] * Ss[i] + upds[i] for i in n]


def _head_norm(o, w):
    return o * lax.rsqrt(jnp.mean(o * o, axis=-1, keepdims=True) + EPS) * w


def _gdn_gate_tile(ab, alog, dtb):
    lane = lax.broadcasted_iota(jnp.int32, ab.shape, 1)
    g = -jnp.exp(alog) * _softplus(ab + dtb)
    return jnp.where(lane < HEADS, g, _sigmoid(ab))


def _l2norm_heads(y, scale):
    outs = []
    for h in range(y.shape[-1] // HEAD_DIM):
        seg = y[:, h * HEAD_DIM:(h + 1) * HEAD_DIM]
        outs.append(seg * (lax.rsqrt(jnp.sum(seg * seg, axis=-1, keepdims=True) + EPS) * scale))
    return jnp.concatenate(outs, axis=-1)


H_TB = 128
G_TB = 128
G_HG = 8


def _causal_conv_silu(x_ref, w_ref, ext_ref):
    tb = x_ref.shape[0]
    ext_ref[SUBLANES:, :] = x_ref[...]
    y = x_ref[...] * w_ref[CONV_W - 1:CONV_W, :]
    for j in range(1, CONV_W):
        y = y + ext_ref[SUBLANES - j:SUBLANES - j + tb, :] * w_ref[CONV_W - 1 - j:CONV_W - j, :]
    ext_ref[0:SUBLANES, :] = x_ref[tb - SUBLANES:, :]
    return _silu(y)


def _hgrn_prompt_chunk(zq, zf, v, gate, lb, nw, s_scr, k_scr, b_scr):
    heads = range(HEADS)
    hc = [slice(h * HEAD_DIM, (h + 1) * HEAD_DIM) for h in heads]
    band = SUBLANES
    q, k, logf = _hgrn_gates(zq, zf, lb)
    b = _cumsum_rows(logf) * LOG2E
    k_scr[...] = k
    b_scr[...] = b
    bl = b[CHUNK - 1:CHUNK, :]
    qe = q * jnp.exp2(b)
    kdec = k * jnp.exp2(bl - b)
    e_last = jnp.exp2(bl)

    lane = lax.broadcasted_iota(jnp.int32, (band, CHUNK), 1)
    row = lax.broadcasted_iota(jnp.int32, (band, CHUNK), 0)
    a_bands = [[None] * (CHUNK // band) for _ in heads]
    for i in range(CHUNK // SUB):
        base = i * SUB
        q_lo, q_hi = q[base:base + band], q[base + band:base + SUB]
        b_lo, b_hi = b[base:base + band], b[base + band:base + SUB]
        if i == 0:
            a_lo = [jnp.zeros((band, CHUNK), F32) for _ in heads]
            a_hi = [jnp.zeros((band, CHUNK), F32) for _ in heads]
        else:
            ref = b[base - 1:base, :]
            qt = q[base:base + SUB] * jnp.exp2(b[base:base + SUB] - ref)
            kt = k * jnp.exp2(jnp.minimum(ref - b, 0.0))
            att = [jnp.where(lane[:1] < base, _bdot_nt(qt[:, hc[h]], kt[:, hc[h]]), 0.0)
                   for h in heads]
            a_lo = [att[h][:band] for h in heads]
            a_hi = [att[h][band:] for h in heads]
        for s in range(SUB):
            r = base + s
            k_s = k_scr[r:r + 1, :]
            b_s = b_scr[r:r + 1, :]
            p_hi = q_hi * k_s * jnp.exp2(jnp.minimum(b_hi - b_s, 0.0))
            if s < band:
                p_lo = q_lo * k_s * jnp.exp2(jnp.minimum(b_lo - b_s, 0.0))
                m_lo = (lane == r) & (row >= s)
                m_hi = lane == r
            else:
                m_hi = (lane == r) & (row >= s - band)
            for h in heads:
                if s < band:
                    a_lo[h] = jnp.where(m_lo, jnp.sum(p_lo[:, hc[h]], axis=-1, keepdims=True),
                                        a_lo[h])
                a_hi[h] = jnp.where(m_hi, jnp.sum(p_hi[:, hc[h]], axis=-1, keepdims=True),
                                    a_hi[h])
        for h in heads:
            a_bands[h][2 * i] = a_lo[h]
            a_bands[h][2 * i + 1] = a_hi[h]

    states = [s_scr[h] for h in heads]
    o_inter = [_bdot(qe[:, hc[h]], states[h]) for h in heads]
    o_intra = [_bdot(jnp.concatenate(a_bands[h], axis=0), v[:, hc[h]]) for h in heads]
    upd = [_bdot_tn(kdec[:, hc[h]], v[:, hc[h]]) for h in heads]
    outs = []
    for h in heads:
        s_scr[h] = _row_to_col(e_last[:, hc[h]]) * states[h] + upd[h]
        outs.append(_head_norm(o_inter[h] + o_intra[h], nw))
    return jnp.concatenate(outs, axis=1) * _sigmoid(gate)


def _hgrn_prompt_body(q_ref, f_ref, i_ref, gate_ref, lb_ref, nw_ref, o_ref, s_ref,
                      s_scr, k_scr, b_scr):
    t = pl.program_id(1)

    @pl.when(t == 0)
    def _():
        s_scr[...] = jnp.zeros_like(s_scr)

    lb = lb_ref[...]
    nw = nw_ref[...]
    for c in range(H_TB // CHUNK):
        rows = slice(c * CHUNK, (c + 1) * CHUNK)
        o = _hgrn_prompt_chunk(q_ref[rows, :], f_ref[rows, :], i_ref[rows, :], gate_ref[rows, :],
                               lb, nw, s_scr, k_scr, b_scr)
        o_ref[rows, :] = o.astype(o_ref.dtype)

    @pl.when(t == pl.num_programs(1) - 1)
    def _():
        s_ref[0] = s_scr[...]


def _hgrn_prompt(proj, lb, hg_norm, layer, batch, seq):
    nt = seq // H_TB
    blk = lambda r: pl.BlockSpec((H_TB, WIDTH), lambda b, t: (b * nt + t, r))
    return pl.pallas_call(
        _hgrn_prompt_body,
        grid=(batch, nt),
        in_specs=[blk(0), blk(1), blk(2), blk(3),
                  pl.BlockSpec((None, 1, WIDTH), lambda b, t: (layer, 0, 0)),
                  pl.BlockSpec((None, 1, HEAD_DIM), lambda b, t: (layer, 0, 0))],
        out_specs=[pl.BlockSpec((H_TB, WIDTH), lambda b, t: (b * nt + t, 0)),
                   pl.BlockSpec((1, HEADS, HEAD_DIM, HEAD_DIM), lambda b, t: (b, 0, 0, 0))],
        out_shape=[jax.ShapeDtypeStruct((batch * seq, WIDTH), BF16),
                   jax.ShapeDtypeStruct((batch, HEADS, HEAD_DIM, HEAD_DIM), F32)],
        scratch_shapes=[pltpu.VMEM((HEADS, HEAD_DIM, HEAD_DIM), F32),
                        pltpu.VMEM((CHUNK, WIDTH), F32),
                        pltpu.VMEM((CHUNK, WIDTH), F32)],
        compiler_params=_params(("parallel", "arbitrary")),
        name="hgrn_prompt",
    )(proj, proj, proj, proj, lb, hg_norm)


def _gdn_prompt_body(xq_ref, xk_ref, xv_ref, wq_ref, wk_ref, wv_ref, z_ref, ab_ref, alog_ref,
                     dtb_ref, nw_ref, o_ref, s_ref, s_scr, q_scr, k_scr, v_scr, cq, ck, cv):
    t = pl.program_id(1)

    @pl.when(t == 0)
    def _():
        s_scr[...] = jnp.zeros_like(s_scr)
        for ext in (cq, ck, cv):
            ext[0:SUBLANES, :] = jnp.zeros((SUBLANES, WIDTH), F32)

    q_scr[...] = _l2norm_heads(_causal_conv_silu(xq_ref, wq_ref, cq), HEAD_DIM ** -0.5)
    k_scr[...] = _l2norm_heads(_causal_conv_silu(xk_ref, wk_ref, ck), 1.0)
    v_scr[...] = _causal_conv_silu(xv_ref, wv_ref, cv)

    nw = nw_ref[...]
    gb = _gdn_gate_tile(ab_ref[...], alog_ref[...], dtb_ref[...])
    nc = G_TB // CHUNK
    d_all = jnp.concatenate(
        [_cumsum_rows(gb[c * CHUNK:(c + 1) * CHUNK]) for c in range(nc)], axis=0)
    d_t = d_all.T
    for c in range(nc):
        rows = slice(c * CHUNK, (c + 1) * CHUNK)
        for h0 in range(0, HEADS, G_HG):
            heads = list(range(h0, h0 + G_HG))
            cols = [slice(h * HEAD_DIM, (h + 1) * HEAD_DIM) for h in heads]
            wy = _gdn_wy([q_scr[rows, cs] for cs in cols],
                         [k_scr[rows, cs] for cs in cols],
                         [v_scr[rows, cs] for cs in cols],
                         [d_all[rows, h:h + 1] for h in heads],
                         [d_t[h:h + 1, rows] for h in heads],
                         [gb[rows, HEADS + h:HEADS + h + 1] for h in heads])
            outs, new_states = _gdn_apply(wy, [s_scr[h] for h in heads])
            for i, h in enumerate(heads):
                s_scr[h] = new_states[i]
                o = _head_norm(outs[i], nw) * _silu(z_ref[rows, cols[i]])
                o_ref[rows, cols[i]] = o.astype(o_ref.dtype)

    @pl.when(t == pl.num_programs(1) - 1)
    def _():
        s_ref[0] = s_scr[...]


def _gdn_prompt(proj, conv_w, z, ab, alog, dtb, gdn_norm, layer, batch, seq):
    nt = seq // G_TB
    qkv_blk = 4 * WIDTH // WIDTH
    blk = lambda r: pl.BlockSpec((G_TB, WIDTH), lambda b, t: (b * nt + t, r))
    cw = lambda c: pl.BlockSpec((None, CONV_W, WIDTH), lambda b, t: (layer, 0, c))
    row = lambda n: pl.BlockSpec((None, 1, n), lambda b, t: (layer, 0, 0))
    return pl.pallas_call(
        _gdn_prompt_body,
        grid=(batch, nt),
        in_specs=[blk(qkv_blk), blk(qkv_blk + 1), blk(qkv_blk + 2), cw(0), cw(1), cw(2), blk(0),
                  pl.BlockSpec((G_TB, LANES), lambda b, t: (b * nt + t, 0)),
                  row(LANES), row(LANES), row(HEAD_DIM)],
        out_specs=[pl.BlockSpec((G_TB, WIDTH), lambda b, t: (b * nt + t, 0)),
                   pl.BlockSpec((1, HEADS, HEAD_DIM, HEAD_DIM), lambda b, t: (b, 0, 0, 0))],
        out_shape=[jax.ShapeDtypeStruct((batch * seq, WIDTH), BF16),
                   jax.ShapeDtypeStruct((batch, HEADS, HEAD_DIM, HEAD_DIM), F32)],
        scratch_shapes=[pltpu.VMEM((HEADS, HEAD_DIM, HEAD_DIM), F32)]
        + [pltpu.VMEM((G_TB, WIDTH), F32)] * 3 + [pltpu.VMEM((SUBLANES + G_TB, WIDTH), F32)] * 3,
        compiler_params=_params(("parallel", "arbitrary")),
        name="gdn_prompt",
    )(proj, proj, proj, conv_w, conv_w, conv_w, z, ab, alog, dtb, gdn_norm)


S_BB = 8
S_T = 4
S_C = 8


def _conv_sample_body(x_ref, buf_ref, w_ref, o_ref):
    c = pl.program_id(1)
    xx = [buf_ref[:, j, :] for j in range(CONV_W - 1)] + [x_ref[:, t, :] for t in range(S_T)]
    for t in range(S_T):
        y = xx[t] * w_ref[0:1, :]
        for j in range(1, CONV_W):
            y = y + xx[t + j] * w_ref[j:j + 1, :]
        y = _silu(y)
        scale = jnp.where(c == 0, HEAD_DIM ** -0.5, 1.0)
        yn = _l2norm_heads(y, scale)
        o_ref[:, t, :] = jnp.where(c == 2, y, yn)


def _conv_sample(proj3, buf, conv_w, layer):
    nb = proj3.shape[0]
    qkv_blk = 4
    return pl.pallas_call(
        _conv_sample_body,
        grid=(nb // S_BB, 3),
        in_specs=[pl.BlockSpec((S_BB, S_T, WIDTH), lambda i, c: (i, 0, qkv_blk + c)),
                  pl.BlockSpec((None, S_BB, CONV_W - 1, WIDTH), lambda i, c: (layer, i, 0, c)),
                  pl.BlockSpec((None, CONV_W, WIDTH), lambda i, c: (layer, 0, c))],
        out_specs=pl.BlockSpec((S_BB, S_T, WIDTH), lambda i, c: (i, 0, c)),
        out_shape=jax.ShapeDtypeStruct((nb, S_T, CONV_DIM), F32),
        compiler_params=_params(("parallel", "arbitrary")),
        name="conv_sample",
    )(proj3, buf, conv_w)


def _load_padded(dst_ref, src_ref):
    dst_ref[...] = jnp.zeros_like(dst_ref)
    dst_ref[0:S_T, :] = src_ref[0]


def _hgrn_sample_body(q_ref, f_ref, i_ref, gate_ref, lb_ref, nw_ref, s0_ref,
                      o_ref, s_ref, qp, fp, ip, gp, op):
    _load_padded(qp, q_ref)
    _load_padded(fp, f_ref)
    _load_padded(ip, i_ref)
    _load_padded(gp, gate_ref)
    nw = nw_ref[...]
    heads = range(HEADS)
    cols = [slice(h * HEAD_DIM, (h + 1) * HEAD_DIM) for h in heads]
    valid = lax.broadcasted_iota(jnp.int32, (S_C, WIDTH), 0) < S_T
    row = lax.broadcasted_iota(jnp.int32, (S_C, 1), 0)
    q, k, logf = _hgrn_gates(qp[...], fp[...], lb_ref[...], valid)
    v = ip[...]
    b = _cumsum_rows(logf)
    bl = b[S_T - 1:S_T, :]
    qe = q * jnp.exp(b)
    kdec = k * jnp.exp(bl - b)
    e_last = jnp.exp(bl)
    o_diag = [jnp.zeros((S_C, HEAD_DIM), F32) for _ in heads]
    for s in range(S_T):
        p = q * k[s:s + 1, :] * jnp.exp(jnp.minimum(b - b[s:s + 1, :], 0.0))
        col = [jnp.where(row >= s, jnp.sum(p[:, cols[h]], axis=-1, keepdims=True), 0.0)
               for h in heads]
        o_diag = [o_diag[h] + col[h] * v[s:s + 1, cols[h]] for h in heads]
    s0 = [s0_ref[0, h] for h in heads]
    o_inter = [_bdot(qe[:, cols[h]], s0[h]) for h in heads]
    upd = [_bdot_tn(kdec[:, cols[h]], v[:, cols[h]]) for h in heads]
    for h in heads:
        s_ref[0, h] = _row_to_col(e_last[:, cols[h]]) * s0[h] + upd[h]
        op[:, cols[h]] = _head_norm(o_inter[h] + o_diag[h], nw)
    o_ref[0] = (op[...] * _sigmoid(gp[...]))[0:S_T, :]


def _hgrn_sample(proj3, lb, hg_norm, s0, layer):
    nb = proj3.shape[0]
    blk = lambda r: pl.BlockSpec((1, S_T, WIDTH), lambda b: (b, 0, r))
    st = pl.BlockSpec((None, 1, HEADS, HEAD_DIM, HEAD_DIM), lambda b: (layer, b, 0, 0, 0))
    return pl.pallas_call(
        _hgrn_sample_body,
        grid=(nb,),
        in_specs=[blk(0), blk(1), blk(2), blk(3),
                  pl.BlockSpec((None, 1, WIDTH), lambda b: (layer, 0, 0)),
                  pl.BlockSpec((None, 1, HEAD_DIM), lambda b: (layer, 0, 0)),
                  st],
        out_specs=[pl.BlockSpec((1, S_T, WIDTH), lambda b: (b, 0, 0)),
                   pl.BlockSpec((1, HEADS, HEAD_DIM, HEAD_DIM), lambda b: (b, 0, 0, 0))],
        out_shape=[jax.ShapeDtypeStruct((nb, S_T, WIDTH), F32),
                   jax.ShapeDtypeStruct((nb, HEADS, HEAD_DIM, HEAD_DIM), F32)],
        scratch_shapes=[pltpu.VMEM((S_C, WIDTH), F32)] * 5,
        compiler_params=_params(("parallel",)),
        name="hgrn_sample",
    )(proj3, proj3, proj3, proj3, lb, hg_norm, s0)


def _gdn_sample_body(q_ref, k_ref, v_ref, z_ref, ab_ref, alog_ref, dtb_ref, nw_ref, s0_ref,
                     o_ref, s_ref, qp, kp, vp, zp, abp, op):
    _load_padded(qp, q_ref)
    _load_padded(kp, k_ref)
    _load_padded(vp, v_ref)
    _load_padded(zp, z_ref)
    _load_padded(abp, ab_ref)
    nw = nw_ref[...]
    valid = lax.broadcasted_iota(jnp.int32, (S_C, LANES), 0) < S_T
    gb = jnp.where(valid, _gdn_gate_tile(abp[...], alog_ref[...], dtb_ref[...]), 0.0)
    d_all = _cumsum_rows(gb)
    row = lax.broadcasted_iota(jnp.int32, (S_C, 1), 0)
    heads = range(HEADS)
    cols = [slice(h * HEAD_DIM, (h + 1) * HEAD_DIM) for h in heads]
    qw, kw = qp[...], kp[...]
    qs = [qw[:, c] for c in cols]
    ks = [kw[:, c] for c in cols]
    ds = [d_all[:, h:h + 1] for h in heads]
    betas = [gb[:, HEADS + h:HEADS + h + 1] for h in heads]
    eds = [jnp.exp(d) for d in ds]
    Xs = [jnp.concatenate([vp[:, cols[h]] * betas[h], ks[h] * (betas[h] * eds[h])], axis=1)
          for h in heads]
    qkl = [[None] * S_T for _ in heads]
    for s in range(S_T):
        qk_w = qw * kw[s:s + 1, :]
        kk_w = kw * kw[s:s + 1, :]
        decay = [jnp.where(row >= s, jnp.exp(jnp.where(row >= s, ds[h] - ds[h][s:s + 1, :], 0.0)),
                           0.0) for h in heads]
        qk_s = [jnp.sum(qk_w[:, cols[h]], axis=-1, keepdims=True) for h in heads]
        for h in heads:
            qkl[h][s] = qk_s[h] * decay[h]
        if s < S_T - 1:
            kk_s = [jnp.sum(kk_w[:, cols[h]], axis=-1, keepdims=True) for h in heads]
            m_s = [jnp.where(row > s, betas[h] * kk_s[h] * decay[h], 0.0) for h in heads]
            Xs = [Xs[h] - m_s[h] * Xs[h][s:s + 1, :] for h in heads]
    d_last = [d[S_T - 1:S_T, :] for d in ds]
    qes = [qs[h] * eds[h] for h in heads]
    kdecs = [ks[h] * jnp.exp(d_last[h] - ds[h]) for h in heads]
    s0 = [s0_ref[0, h] for h in heads]
    sws = [_bdot(jnp.concatenate([Xs[h][:, HEAD_DIM:], qes[h]], axis=0), s0[h]) for h in heads]
    v_news = [Xs[h][:, :HEAD_DIM] - sws[h][:S_C] for h in heads]
    upds = [_bdot_tn(kdecs[h], v_news[h]) for h in heads]
    for h in heads:
        o = sws[h][S_C:]
        for s in range(S_T):
            o = o + qkl[h][s] * v_news[h][s:s + 1, :]
        s_ref[0, h] = jnp.exp(d_last[h]) * s0[h] + upds[h]
        op[:, cols[h]] = _head_norm(o, nw) * _silu(zp[:, cols[h]])
    o_ref[0] = op[0:S_T, :]


def _gdn_sample(qkv3, z3, ab3, alog, dtb, gdn_norm, s0, layer):
    nb = qkv3.shape[0]
    blk = lambda r: pl.BlockSpec((1, S_T, WIDTH), lambda b: (b, 0, r))
    return pl.pallas_call(
        _gdn_sample_body,
        grid=(nb,),
        in_specs=[blk(0), blk(1), blk(2), blk(0),
                  pl.BlockSpec((1, S_T, LANES), lambda b: (b, 0, 0)),
                  pl.BlockSpec((None, 1, LANES), lambda b: (layer, 0, 0)),
                  pl.BlockSpec((None, 1, LANES), lambda b: (layer, 0, 0)),
                  pl.BlockSpec((None, 1, HEAD_DIM), lambda b: (layer, 0, 0)),
                  pl.BlockSpec((None, 1, HEADS, HEAD_DIM, HEAD_DIM), lambda b: (layer, b, 0, 0, 0))],
        out_specs=[pl.BlockSpec((1, S_T, WIDTH), lambda b: (b, 0, 0)),
                   pl.BlockSpec((1, HEADS, HEAD_DIM, HEAD_DIM), lambda b: (b, 0, 0, 0))],
        out_shape=[jax.ShapeDtypeStruct((nb, S_T, WIDTH), F32),
                   jax.ShapeDtypeStruct((nb, HEADS, HEAD_DIM, HEAD_DIM), F32)],
        scratch_shapes=[pltpu.VMEM((S_C, WIDTH), F32)] * 4 + [pltpu.VMEM((S_C, LANES), F32),
                                                              pltpu.VMEM((S_C, WIDTH), F32)],
        compiler_params=_params(("parallel",)),
        name="gdn_sample",
    )(qkv3, qkv3, qkv3, z3, ab3, alog, dtb, gdn_norm, s0)


def _trunk(x, sample, states, W):
    rows = x.shape[0]
    tm = 512 if sample else 1024
    tm_ff = 512 if sample else 2048
    hg_states, gdn_states, conv_states = [], [], []
    for l in range(DEPTH):
        xn = _rmsnorm(x, W["ffn1_norm"][l], BF16)
        a = _swiglu_up(xn, W["ffn1_w_gu"], l, tm_ff)
        x = _down_residual(a, W["ffn1_w_down"], l, x, 0.5, 512, 512)

        xn = _rmsnorm(x, W["mix_norm"][l], BF16)
        proj = _matmul(xn, W["w_in"], l, tm, 512, MAIN_COLS)
        z = _matmul(xn, W["w_in_z"], l, tm, 1024, WIDTH)
        ab = _matmul(xn, W["w_in_ab"], l, tm, LANES, LANES)
        if sample:
            nb = rows // S_T
            proj3 = proj.reshape(nb, S_T, MAIN_COLS)
            ab3 = ab.reshape(nb, S_T, LANES)
            qkv3 = _conv_sample(proj3, states["conv"], W["conv_w"], l)
            o_hg, s_hg = _hgrn_sample(proj3, W["lb"], W["hg_norm"], states["hgrn"], l)
            o_gdn, s_gdn = _gdn_sample(qkv3, z.reshape(nb, S_T, WIDTH), ab3, W["alog"], W["dtb"],
                                       W["gdn_norm"], states["gdn"], l)
            o_hg = o_hg.reshape(rows, WIDTH)
            o_gdn = o_gdn.reshape(rows, WIDTH)
            conv_new = proj3[:, S_T - (CONV_W - 1):, 4 * WIDTH:4 * WIDTH + CONV_DIM]
        else:
            batch, seq = states["batch"], states["seq"]
            o_hg, s_hg = _hgrn_prompt(proj, W["lb"], W["hg_norm"], l, batch, seq)
            o_gdn, s_gdn = _gdn_prompt(proj, W["conv_w"], z, ab, W["alog"], W["dtb"],
                                       W["gdn_norm"], l, batch, seq)
            conv_new = proj.reshape(batch, seq, MAIN_COLS)[
                :, seq - (CONV_W - 1):, 4 * WIDTH:4 * WIDTH + CONV_DIM]
        x = _mix_out_residual(o_hg, o_gdn, W["w_out"], l, x, tm, 512)

        xn = _rmsnorm(x, W["ffn2_norm"][l], BF16)
        a = _swiglu_up(xn, W["ffn2_w_gu"], l, tm_ff)
        x = _down_residual(a, W["ffn2_w_down"], l, x, 0.5, 512, 512)

        hg_states.append(s_hg)
        gdn_states.append(s_gdn)
        conv_states.append(conv_new)
    y = _rmsnorm(x, W["final_norm"], F32)
    return y, jnp.stack(hg_states), jnp.stack(gdn_states), jnp.stack(conv_states)


def kernel(x_prompt, x_sample, state_hgrn, state_gdn, state_conv, ffn1_norm, ffn1_w_gu, ffn1_w_down, mix_norm, w_in, hg_lower_bounds, hg_norm, gdn_conv_w, gdn_A_log, gdn_dt_bias, gdn_norm, w_out, ffn2_norm, ffn2_w_gu, ffn2_w_down, final_norm):
    batch, seq, _ = x_prompt.shape
    dec_batch, dec_seq, _ = x_sample.shape
    assert dec_seq == S_T and seq % H_TB == 0 and seq % G_TB == 0

    ab_lo = 4 * WIDTH + CONV_DIM
    w_tail = w_in[:, :, ab_lo:].astype(BF16)
    row3 = lambda v: v.astype(F32)[:, None, :]
    pad_row = lambda v: row3(jnp.pad(v, ((0, 0), (0, LANES - v.shape[1]))))
    W = {
        "ffn1_norm": ffn1_norm, "mix_norm": mix_norm, "ffn2_norm": ffn2_norm,
        "final_norm": final_norm, "hg_norm": row3(hg_norm), "gdn_norm": row3(gdn_norm),
        "conv_w": gdn_conv_w,
        "ffn1_w_gu": ffn1_w_gu, "ffn1_w_down": ffn1_w_down.astype(BF16),
        "ffn2_w_gu": ffn2_w_gu, "ffn2_w_down": ffn2_w_down.astype(BF16),
        "w_out": w_out,
        "w_in": w_in[:, :, :ab_lo],
        "w_in_z": w_tail[:, :, 2 * HEADS:],
        "w_in_ab": jnp.pad(w_tail[:, :, :2 * HEADS], ((0, 0), (0, 0), (0, LANES - 2 * HEADS))),
        "alog": pad_row(gdn_A_log), "dtb": pad_row(gdn_dt_bias),
        "lb": row3(_lower_bounds(hg_lower_bounds)),
    }

    y_p, hg_p, gdn_p, conv_p = _trunk(
        x_prompt.reshape(batch * seq, D_MODEL), False, {"batch": batch, "seq": seq}, W)
    y_s, hg_s, gdn_s, conv_s = _trunk(
        x_sample.reshape(dec_batch * dec_seq, D_MODEL), True,
        {"hgrn": state_hgrn, "gdn": state_gdn, "conv": state_conv}, W)
    return (y_p.reshape(batch, seq, D_MODEL), y_s.reshape(dec_batch, dec_seq, D_MODEL),
            hg_p, gdn_p, conv_p, hg_s, gdn_s, conv_s)
```

```python
import functools

import jax
import jax.numpy as jnp
from jax import lax
from jax.experimental import pallas as pl
from jax.experimental.pallas import tpu as pltpu

D_MODEL = 4096
DEPTH = 4
HEAD_DIM = 128
HEADS = 16
WIDTH = HEADS * HEAD_DIM
CONV_W = 4
CONV_DIM = 3 * WIDTH
D_FF = 11008
EPS = 1e-6
F_FLOOR = 1e-20
LOG2E = 1.4426950408889634
LANES = 128
SUBLANES = 8
FF_TILE = 256
CHUNK = 64
SUB = 16
MAIN_COLS = 4 * WIDTH + CONV_DIM
VMEM_LIMIT = 56 * 1024 * 1024

BF16 = jnp.bfloat16
F32 = jnp.float32


def _params(sem, vmem=VMEM_LIMIT):
    return pltpu.CompilerParams(dimension_semantics=sem, vmem_limit_bytes=vmem)


def _bdot(a, b):
    return jnp.dot(a.astype(BF16), b.astype(BF16), preferred_element_type=F32)


def _bdot_nt(a, b):
    return lax.dot_general(a.astype(BF16), b.astype(BF16), (((1,), (1,)), ((), ())),
                           preferred_element_type=F32)


def _bdot_tn(a, b):
    return lax.dot_general(a.astype(BF16), b.astype(BF16), (((0,), (0,)), ((), ())),
                           preferred_element_type=F32)


def _sigmoid(x):
    return 1.0 / (1.0 + jnp.exp(-x))


def _silu(x):
    return x * _sigmoid(x)


def _softplus(x):
    return jnp.maximum(x, 0.0) + jnp.log(1.0 + jnp.exp(-jnp.abs(x)))


def _cumsum_rows(x):
    n = x.shape[0]
    row = lax.broadcasted_iota(jnp.int32, x.shape, 0)
    s = 1
    while s < n:
        x = x + jnp.where(row >= s, pltpu.roll(x, s, axis=0), 0.0)
        s *= 2
    return x


def _row_to_col(r):
    n = r.shape[1]
    ri = lax.broadcasted_iota(jnp.int32, (n, n), 0)
    ci = lax.broadcasted_iota(jnp.int32, (n, n), 1)
    return jnp.sum(jnp.where(ri == ci, jnp.broadcast_to(r, (n, n)), 0.0), axis=1, keepdims=True)


def _col_to_row(c):
    n = c.shape[0]
    ri = lax.broadcasted_iota(jnp.int32, (n, n), 0)
    ci = lax.broadcasted_iota(jnp.int32, (n, n), 1)
    return jnp.sum(jnp.where(ri == ci, jnp.broadcast_to(c, (n, n)), 0.0), axis=0, keepdims=True)


def _rmsnorm_body(x_ref, w_ref, o_ref):
    x = x_ref[...]
    y = x * lax.rsqrt(jnp.mean(x * x, axis=-1, keepdims=True) + EPS)
    o_ref[...] = (y * w_ref[...]).astype(o_ref.dtype)


def _rmsnorm(x, w, out_dtype, tm=512):
    m = x.shape[0]
    return pl.pallas_call(
        _rmsnorm_body,
        grid=(m // tm,),
        in_specs=[pl.BlockSpec((tm, D_MODEL), lambda i: (i, 0)),
                  pl.BlockSpec((1, D_MODEL), lambda i: (0, 0))],
        out_specs=pl.BlockSpec((tm, D_MODEL), lambda i: (i, 0)),
        out_shape=jax.ShapeDtypeStruct((m, D_MODEL), out_dtype),
        compiler_params=_params(("parallel",)),
        name="rmsnorm",
    )(x, w.reshape(1, D_MODEL))


def _mm_nt_body(x_ref, w_ref, o_ref):
    o_ref[...] = lax.dot_general(x_ref[...], w_ref[...].astype(BF16), (((1,), (1,)), ((), ())),
                                 preferred_element_type=F32)


def _matmul_nt(xn, w_t, layer, tm, tn, blk0, n):
    m, k = xn.shape
    return pl.pallas_call(
        _mm_nt_body,
        grid=(m // tm, n // tn),
        in_specs=[pl.BlockSpec((tm, k), lambda i, j: (i, 0)),
                  pl.BlockSpec((None, tn, k), lambda i, j: (layer, blk0 + j, 0))],
        out_specs=pl.BlockSpec((tm, tn), lambda i, j: (i, j)),
        out_shape=jax.ShapeDtypeStruct((m, n), F32),
        compiler_params=_params(("parallel", "arbitrary")),
        name="matmul_nt",
    )(xn, w_t)


def _swiglu_body(x_ref, wg_ref, wu_ref, o_ref):
    w = jnp.concatenate([wg_ref[...].astype(BF16), wu_ref[...].astype(BF16)], axis=1)
    gu = jnp.dot(x_ref[...], w, preferred_element_type=F32)
    o_ref[...] = (_silu(gu[:, :FF_TILE]) * gu[:, FF_TILE:]).astype(o_ref.dtype)


def _swiglu_up(xn, w_gu, layer, tm):
    m, k = xn.shape
    nj = D_FF // FF_TILE
    return pl.pallas_call(
        _swiglu_body,
        grid=(m // tm, nj),
        in_specs=[pl.BlockSpec((tm, k), lambda i, j: (i, 0), pipeline_mode=pl.Buffered(1)),
                  pl.BlockSpec((None, k, FF_TILE), lambda i, j: (layer, 0, j)),
                  pl.BlockSpec((None, k, FF_TILE), lambda i, j: (layer, 0, j + nj))],
        out_specs=pl.BlockSpec((tm, FF_TILE), lambda i, j: (i, j)),
        out_shape=jax.ShapeDtypeStruct((m, D_FF), BF16),
        compiler_params=_params(("parallel", "arbitrary")),
        name="swiglu_up",
    )(xn, w_gu, w_gu)


def _down_body(scale, a_ref, w_ref, r_ref, o_ref):
    acc = jnp.dot(a_ref[...], w_ref[...], preferred_element_type=F32)
    o_ref[...] = r_ref[...] + scale * acc


def _down_residual(a, w, layer, res, scale, tm, tn):
    m, k = a.shape
    return pl.pallas_call(
        functools.partial(_down_body, scale),
        grid=(m // tm, D_MODEL // tn),
        in_specs=[pl.BlockSpec((tm, k), lambda i, j: (i, 0)),
                  pl.BlockSpec((None, k, tn), lambda i, j: (layer, 0, j)),
                  pl.BlockSpec((tm, tn), lambda i, j: (i, j))],
        out_specs=pl.BlockSpec((tm, tn), lambda i, j: (i, j)),
        out_shape=jax.ShapeDtypeStruct((m, D_MODEL), F32),
        compiler_params=_params(("parallel", "arbitrary")),
        name="down_residual",
    )(a, w, res)


def _mixout_body(a1_ref, a2_ref, w1_ref, w2_ref, r_ref, o_ref):
    acc = jnp.dot(a1_ref[...].astype(BF16), w1_ref[...].astype(BF16), preferred_element_type=F32)
    acc += jnp.dot(a2_ref[...].astype(BF16), w2_ref[...].astype(BF16), preferred_element_type=F32)
    o_ref[...] = r_ref[...] + acc


def _mix_out_residual(o_hg, o_gdn, w_out, layer, res, tm, tn):
    m = o_hg.shape[0]
    return pl.pallas_call(
        _mixout_body,
        grid=(m // tm, D_MODEL // tn),
        in_specs=[pl.BlockSpec((tm, WIDTH), lambda i, j: (i, 0)),
                  pl.BlockSpec((tm, WIDTH), lambda i, j: (i, 0)),
                  pl.BlockSpec((None, WIDTH, tn), lambda i, j: (layer, 0, j)),
                  pl.BlockSpec((None, WIDTH, tn), lambda i, j: (layer, 1, j)),
                  pl.BlockSpec((tm, tn), lambda i, j: (i, j))],
        out_specs=pl.BlockSpec((tm, tn), lambda i, j: (i, j)),
        out_shape=jax.ShapeDtypeStruct((m, D_MODEL), F32),
        compiler_params=_params(("parallel", "arbitrary")),
        name="mix_out_residual",
    )(o_hg, o_gdn, w_out, w_out, res)


def _lb_body(p_ref, o_ref):
    x = p_ref[...]
    e = jnp.exp(x - jnp.max(x, axis=0, keepdims=True))
    p = e / jnp.sum(e, axis=0, keepdims=True)
    rows = [p[0:1]]
    for l in range(1, DEPTH):
        rows.append(rows[-1] + p[l:l + 1])
    cs = jnp.concatenate(rows, axis=0)
    o_ref[...] = jnp.clip(cs - p[0:1], 0.0, 1.0)


def _lower_bounds(hg_lower_bounds):
    return pl.pallas_call(
        _lb_body,
        out_shape=jax.ShapeDtypeStruct((DEPTH, WIDTH), F32),
        name="lower_bounds",
    )(hg_lower_bounds)


def _hgrn_gates(zq, zf, lb, valid=None):
    q = _silu(zq)
    sig = _sigmoid(zf)
    f = lb + (1.0 - lb) * sig
    logf = jnp.log(jnp.maximum(f, F_FLOOR))
    k = (1.0 - lb) * (1.0 - sig)
    if valid is not None:
        logf = jnp.where(valid, logf, 0.0)
        k = jnp.where(valid, k, 0.0)
    return q, k, logf


def _gdn_wy(qs, ks, vs, d_cols, d_rows, beta_cols):
    c = qs[0].shape[0]
    n = range(len(qs))
    ri = lax.broadcasted_iota(jnp.int32, (c, c), 0)
    ci = lax.broadcasted_iota(jnp.int32, (c, c), 1)
    causal = ri >= ci
    strict = ri > ci
    Ls = [jnp.where(causal, jnp.exp(jnp.where(causal, d_cols[i] - d_rows[i], 0.0)), 0.0) for i in n]
    kks = [_bdot_nt(ks[i], ks[i]) for i in n]
    qks = [_bdot_nt(qs[i], ks[i]) for i in n]
    Ps = [jnp.where(strict, -(beta_cols[i] * kks[i] * Ls[i]), 0.0) for i in n]
    Tms = Ps
    span = 2
    while span < c:
        Ps = [_bdot(Ps[i], Ps[i]) for i in n]
        TPs = [_bdot(Tms[i], Ps[i]) for i in n]
        Tms = [Tms[i] + Ps[i] + TPs[i] for i in n]
        span *= 2
    eds = [jnp.exp(d_cols[i]) for i in n]
    rhss = [jnp.concatenate([vs[i] * beta_cols[i], ks[i] * (beta_cols[i] * eds[i])], axis=1)
            for i in n]
    Xs = [rhss[i] + _bdot(Tms[i], rhss[i]) for i in n]
    out = []
    for i in n:
        d_last = d_cols[i][c - 1:c, :]
        out.append(dict(u=Xs[i][:, :HEAD_DIM], w=Xs[i][:, HEAD_DIM:], qkl=qks[i] * Ls[i],
                        qe=qs[i] * eds[i], kdec=ks[i] * jnp.exp(d_last - d_cols[i]),
                        e_last=jnp.exp(d_last)))
    return out


def _gdn_apply(wy, Ss):
    n = range(len(wy))
    c = wy[0]["u"].shape[0]
    sws = [_bdot(jnp.concatenate([wy[i]["w"], wy[i]["qe"]], axis=0), Ss[i]) for i in n]
    v_news = [wy[i]["u"] - sws[i][:c] for i in n]
    o2s = [_bdot(wy[i]["qkl"], v_news[i]) for i in n]
    upds = [_bdot_tn(wy[i]["kdec"], v_news[i]) for i in n]
    outs = [sws[i][c:] + o2s[i] for i in n]
    return outs, [wy[i]["e_last"] * Ss[i] + upds[i] for i in n]


def _head_norm(o, w):
    return o * lax.rsqrt(jnp.mean(o * o, axis=-1, keepdims=True) + EPS) * w


def _gdn_gate_tile(ab, alog, dtb):
    lane = lax.broadcasted_iota(jnp.int32, ab.shape, 1)
    g = -jnp.exp(alog) * _softplus(ab + dtb)
    return jnp.where(lane < HEADS, g, _sigmoid(ab))


def _l2norm_heads(y, scale):
    outs = []
    for h in range(y.shape[-1] // HEAD_DIM):
        seg = y[:, h * HEAD_DIM:(h + 1) * HEAD_DIM]
        outs.append(seg * (lax.rsqrt(jnp.sum(seg * seg, axis=-1, keepdims=True) + EPS) * scale))
    return jnp.concatenate(outs, axis=-1)


H_TB = 128
G_TB = 128
G_HG = 8


def _causal_conv_silu(x_ref, w_ref, ext_ref):
    tb = x_ref.shape[0]
    ext_ref[SUBLANES:, :] = x_ref[...]
    y = x_ref[...] * w_ref[CONV_W - 1:CONV_W, :]
    for j in range(1, CONV_W):
        y = y + ext_ref[SUBLANES - j:SUBLANES - j + tb, :] * w_ref[CONV_W - 1 - j:CONV_W - j, :]
    ext_ref[0:SUBLANES, :] = x_ref[tb - SUBLANES:, :]
    return _silu(y)


def _hgrn_prompt_chunk(zq, zf, v, gate, lb, nw, s_scr, k_scr, b_scr):
    heads = range(HEADS)
    hc = [slice(h * HEAD_DIM, (h + 1) * HEAD_DIM) for h in heads]
    band = SUBLANES
    q, k, logf = _hgrn_gates(zq, zf, lb)
    b = _cumsum_rows(logf) * LOG2E
    k_scr[...] = k
    b_scr[...] = b
    bl = b[CHUNK - 1:CHUNK, :]
    qe = q * jnp.exp2(b)
    kdec = k * jnp.exp2(bl - b)
    e_last = jnp.exp2(bl)

    lane = lax.broadcasted_iota(jnp.int32, (band, CHUNK), 1)
    row = lax.broadcasted_iota(jnp.int32, (band, CHUNK), 0)
    a_bands = [[None] * (CHUNK // band) for _ in heads]
    for i in range(CHUNK // SUB):
        base = i * SUB
        q_lo, q_hi = q[base:base + band], q[base + band:base + SUB]
        b_lo, b_hi = b[base:base + band], b[base + band:base + SUB]
        if i == 0:
            a_lo = [jnp.zeros((band, CHUNK), F32) for _ in heads]
            a_hi = [jnp.zeros((band, CHUNK), F32) for _ in heads]
        else:
            ref = b[base - 1:base, :]
            qt = q[base:base + SUB] * jnp.exp2(b[base:base + SUB] - ref)
            kt = k * jnp.exp2(jnp.minimum(ref - b, 0.0))
            att = [jnp.where(lane[:1] < base, _bdot_nt(qt[:, hc[h]], kt[:, hc[h]]), 0.0)
                   for h in heads]
            a_lo = [att[h][:band] for h in heads]
            a_hi = [att[h][band:] for h in heads]
        for s in range(SUB):
            r = base + s
            k_s = k_scr[r:r + 1, :]
            b_s = b_scr[r:r + 1, :]
            p_hi = q_hi * k_s * jnp.exp2(jnp.minimum(b_hi - b_s, 0.0))
            if s < band:
                p_lo = q_lo * k_s * jnp.exp2(jnp.minimum(b_lo - b_s, 0.0))
                m_lo = (lane == r) & (row >= s)
                m_hi = lane == r
            else:
                m_hi = (lane == r) & (row >= s - band)
            for h in heads:
                if s < band:
                    a_lo[h] = jnp.where(m_lo, jnp.sum(p_lo[:, hc[h]], axis=-1, keepdims=True),
                                        a_lo[h])
                a_hi[h] = jnp.where(m_hi, jnp.sum(p_hi[:, hc[h]], axis=-1, keepdims=True),
                                    a_hi[h])
        for h in heads:
            a_bands[h][2 * i] = a_lo[h]
            a_bands[h][2 * i + 1] = a_hi[h]

    states = [s_scr[h] for h in heads]
    o_inter = [_bdot(qe[:, hc[h]], states[h]) for h in heads]
    o_intra = [_bdot(jnp.concatenate(a_bands[h], axis=0), v[:, hc[h]]) for h in heads]
    upd = [_bdot_tn(kdec[:, hc[h]], v[:, hc[h]]) for h in heads]
    outs = []
    for h in heads:
        s_scr[h] = _row_to_col(e_last[:, hc[h]]) * states[h] + upd[h]
        outs.append(_head_norm(o_inter[h] + o_intra[h], nw))
    return jnp.concatenate(outs, axis=1) * _sigmoid(gate)


def _hgrn_prompt_body(q_ref, f_ref, i_ref, gate_ref, lb_ref, nw_ref, o_ref, s_ref,
                      s_scr, k_scr, b_scr):
    t = pl.program_id(1)

    @pl.when(t == 0)
    def _():
        s_scr[...] = jnp.zeros_like(s_scr)

    lb = lb_ref[...]
    nw = nw_ref[...]
    for c in range(H_TB // CHUNK):
        rows = slice(c * CHUNK, (c + 1) * CHUNK)
        o = _hgrn_prompt_chunk(q_ref[rows, :], f_ref[rows, :], i_ref[rows, :], gate_ref[rows, :],
                               lb, nw, s_scr, k_scr, b_scr)
        o_ref[rows, :] = o.astype(o_ref.dtype)

    @pl.when(t == pl.num_programs(1) - 1)
    def _():
        s_ref[0] = s_scr[...]


def _hgrn_prompt(proj, lb, hg_norm, layer, batch, seq):
    nt = seq // H_TB
    blk = lambda r: pl.BlockSpec((H_TB, WIDTH), lambda b, t: (b * nt + t, r))
    return pl.pallas_call(
        _hgrn_prompt_body,
        grid=(batch, nt),
        in_specs=[blk(0), blk(1), blk(2), blk(3),
                  pl.BlockSpec((None, 1, WIDTH), lambda b, t: (layer, 0, 0)),
                  pl.BlockSpec((None, 1, HEAD_DIM), lambda b, t: (layer, 0, 0))],
        out_specs=[pl.BlockSpec((H_TB, WIDTH), lambda b, t: (b * nt + t, 0)),
                   pl.BlockSpec((1, HEADS, HEAD_DIM, HEAD_DIM), lambda b, t: (b, 0, 0, 0))],
        out_shape=[jax.ShapeDtypeStruct((batch * seq, WIDTH), BF16),
                   jax.ShapeDtypeStruct((batch, HEADS, HEAD_DIM, HEAD_DIM), F32)],
        scratch_shapes=[pltpu.VMEM((HEADS, HEAD_DIM, HEAD_DIM), F32),
                        pltpu.VMEM((CHUNK, WIDTH), F32),
                        pltpu.VMEM((CHUNK, WIDTH), F32)],
        compiler_params=_params(("parallel", "arbitrary")),
        name="hgrn_prompt",
    )(proj, proj, proj, proj, lb, hg_norm)


def _gdn_prompt_body(xq_ref, xk_ref, xv_ref, wq_ref, wk_ref, wv_ref, z_ref, ab_ref, alog_ref,
                     dtb_ref, nw_ref, o_ref, s_ref, s_scr, q_scr, k_scr, v_scr, cq, ck, cv):
    t = pl.program_id(1)

    @pl.when(t == 0)
    def _():
        s_scr[...] = jnp.zeros_like(s_scr)
        for ext in (cq, ck, cv):
            ext[0:SUBLANES, :] = jnp.zeros((SUBLANES, WIDTH), F32)

    q_scr[...] = _l2norm_heads(_causal_conv_silu(xq_ref, wq_ref, cq), HEAD_DIM ** -0.5)
    k_scr[...] = _l2norm_heads(_causal_conv_silu(xk_ref, wk_ref, ck), 1.0)
    v_scr[...] = _causal_conv_silu(xv_ref, wv_ref, cv)

    nw = nw_ref[...]
    gb = _gdn_gate_tile(ab_ref[...], alog_ref[...], dtb_ref[...])
    nc = G_TB // CHUNK
    d_all = jnp.concatenate(
        [_cumsum_rows(gb[c * CHUNK:(c + 1) * CHUNK]) for c in range(nc)], axis=0)
    d_t = d_all.T
    for c in range(nc):
        rows = slice(c * CHUNK, (c + 1) * CHUNK)
        for h0 in range(0, HEADS, G_HG):
            heads = list(range(h0, h0 + G_HG))
            cols = [slice(h * HEAD_DIM, (h + 1) * HEAD_DIM) for h in heads]
            wy = _gdn_wy([q_scr[rows, cs] for cs in cols],
                         [k_scr[rows, cs] for cs in cols],
                         [v_scr[rows, cs] for cs in cols],
                         [d_all[rows, h:h + 1] for h in heads],
                         [d_t[h:h + 1, rows] for h in heads],
                         [gb[rows, HEADS + h:HEADS + h + 1] for h in heads])
            outs, new_states = _gdn_apply(wy, [s_scr[h] for h in heads])
            for i, h in enumerate(heads):
                s_scr[h] = new_states[i]
                o = _head_norm(outs[i], nw) * _silu(z_ref[rows, cols[i]])
                o_ref[rows, cols[i]] = o.astype(o_ref.dtype)

    @pl.when(t == pl.num_programs(1) - 1)
    def _():
        s_ref[0] = s_scr[...]


def _gdn_prompt(proj, conv_w, z, ab, alog, dtb, gdn_norm, layer, batch, seq):
    nt = seq // G_TB
    qkv_blk = 4 * WIDTH // WIDTH
    blk = lambda r: pl.BlockSpec((G_TB, WIDTH), lambda b, t: (b * nt + t, r))
    cw = lambda c: pl.BlockSpec((None, CONV_W, WIDTH), lambda b, t: (layer, 0, c))
    row = lambda n: pl.BlockSpec((None, 1, n), lambda b, t: (layer, 0, 0))
    return pl.pallas_call(
        _gdn_prompt_body,
        grid=(batch, nt),
        in_specs=[blk(qkv_blk), blk(qkv_blk + 1), blk(qkv_blk + 2), cw(0), cw(1), cw(2), blk(0),
                  pl.BlockSpec((G_TB, LANES), lambda b, t: (b * nt + t, 0)),
                  row(LANES), row(LANES), row(HEAD_DIM)],
        out_specs=[pl.BlockSpec((G_TB, WIDTH), lambda b, t: (b * nt + t, 0)),
                   pl.BlockSpec((1, HEADS, HEAD_DIM, HEAD_DIM), lambda b, t: (b, 0, 0, 0))],
        out_shape=[jax.ShapeDtypeStruct((batch * seq, WIDTH), BF16),
                   jax.ShapeDtypeStruct((batch, HEADS, HEAD_DIM, HEAD_DIM), F32)],
        scratch_shapes=[pltpu.VMEM((HEADS, HEAD_DIM, HEAD_DIM), F32)]
        + [pltpu.VMEM((G_TB, WIDTH), F32)] * 3 + [pltpu.VMEM((SUBLANES + G_TB, WIDTH), F32)] * 3,
        compiler_params=_params(("parallel", "arbitrary")),
        name="gdn_prompt",
    )(proj, proj, proj, conv_w, conv_w, conv_w, z, ab, alog, dtb, gdn_norm)


S_BB = 8
S_T = 4
S_C = 8


def _conv_sample_body(x_ref, buf_ref, w_ref, o_ref):
    c = pl.program_id(1)
    xx = [buf_ref[:, j, :] for j in range(CONV_W - 1)] + [x_ref[:, t, :] for t in range(S_T)]
    for t in range(S_T):
        y = xx[t] * w_ref[0:1, :]
        for j in range(1, CONV_W):
            y = y + xx[t + j] * w_ref[j:j + 1, :]
        y = _silu(y)
        scale = jnp.where(c == 0, HEAD_DIM ** -0.5, 1.0)
        yn = _l2norm_heads(y, scale)
        o_ref[:, t, :] = jnp.where(c == 2, y, yn)


def _conv_sample(proj3, buf, conv_w, layer):
    nb = proj3.shape[0]
    qkv_blk = 4
    return pl.pallas_call(
        _conv_sample_body,
        grid=(nb // S_BB, 3),
        in_specs=[pl.BlockSpec((S_BB, S_T, WIDTH), lambda i, c: (i, 0, qkv_blk + c)),
                  pl.BlockSpec((None, S_BB, CONV_W - 1, WIDTH), lambda i, c: (layer, i, 0, c)),
                  pl.BlockSpec((None, CONV_W, WIDTH), lambda i, c: (layer, 0, c))],
        out_specs=pl.BlockSpec((S_BB, S_T, WIDTH), lambda i, c: (i, 0, c)),
        out_shape=jax.ShapeDtypeStruct((nb, S_T, CONV_DIM), F32),
        compiler_params=_params(("parallel", "arbitrary")),
        name="conv_sample",
    )(proj3, buf, conv_w)


def _load_padded(dst_ref, src_ref):
    dst_ref[...] = jnp.zeros_like(dst_ref)
    dst_ref[0:S_T, :] = src_ref[0]


def _hgrn_sample_body(q_ref, f_ref, i_ref, gate_ref, lb_ref, nw_ref, s0_ref, _aliased_states,
                      o_ref, s_ref, qp, fp, ip, gp, op):
    _load_padded(qp, q_ref)
    _load_padded(fp, f_ref)
    _load_padded(ip, i_ref)
    _load_padded(gp, gate_ref)
    nw = nw_ref[...]
    heads = range(HEADS)
    cols = [slice(h * HEAD_DIM, (h + 1) * HEAD_DIM) for h in heads]
    valid = lax.broadcasted_iota(jnp.int32, (S_C, WIDTH), 0) < S_T
    row = lax.broadcasted_iota(jnp.int32, (S_C, 1), 0)
    q, k, logf = _hgrn_gates(qp[...], fp[...], lb_ref[...], valid)
    v = ip[...]
    b = _cumsum_rows(logf)
    bl = b[S_T - 1:S_T, :]
    qe = q * jnp.exp(b)
    kdec = k * jnp.exp(bl - b)
    e_last = jnp.exp(bl)
    o_diag = [jnp.zeros((S_C, HEAD_DIM), F32) for _ in heads]
    for s in range(S_T):
        p = q * k[s:s + 1, :] * jnp.exp(jnp.minimum(b - b[s:s + 1, :], 0.0))
        col = [jnp.where(row >= s, jnp.sum(p[:, cols[h]], axis=-1, keepdims=True), 0.0)
               for h in heads]
        o_diag = [o_diag[h] + col[h] * v[s:s + 1, cols[h]] for h in heads]
    s0 = [s0_ref[0, h] for h in heads]
    o_inter = [_bdot(qe[:, cols[h]], s0[h]) for h in heads]
    upd = [_bdot_tn(kdec[:, cols[h]], v[:, cols[h]]) for h in heads]
    for h in heads:
        s_ref[0, h] = _row_to_col(e_last[:, cols[h]]) * s0[h] + upd[h]
        op[:, cols[h]] = _head_norm(o_inter[h] + o_diag[h], nw)
    o_ref[0] = (op[...] * _sigmoid(gp[...]))[0:S_T, :]


def _hgrn_sample(proj3, lb, hg_norm, s0, new_states, layer):
    nb = proj3.shape[0]
    blk = lambda r: pl.BlockSpec((1, S_T, WIDTH), lambda b: (b, 0, r))
    st = pl.BlockSpec((None, 1, HEADS, HEAD_DIM, HEAD_DIM), lambda b: (layer, b, 0, 0, 0))
    return pl.pallas_call(
        _hgrn_sample_body,
        grid=(nb,),
        in_specs=[blk(0), blk(1), blk(2), blk(3),
                  pl.BlockSpec((None, 1, WIDTH), lambda b: (layer, 0, 0)),
                  pl.BlockSpec((None, 1, HEAD_DIM), lambda b: (layer, 0, 0)),
                  st, pl.BlockSpec(memory_space=pl.ANY)],
        out_specs=[pl.BlockSpec((1, S_T, WIDTH), lambda b: (b, 0, 0)), st],
        out_shape=[jax.ShapeDtypeStruct((nb, S_T, WIDTH), F32),
                   jax.ShapeDtypeStruct(new_states.shape, F32)],
        input_output_aliases={7: 1},
        scratch_shapes=[pltpu.VMEM((S_C, WIDTH), F32)] * 5,
        compiler_params=_params(("parallel",)),
        name="hgrn_sample",
    )(proj3, proj3, proj3, proj3, lb, hg_norm, s0, new_states)


def _gdn_sample_body(q_ref, k_ref, v_ref, z_ref, ab_ref, alog_ref, dtb_ref, nw_ref, s0_ref,
                     _aliased_states, o_ref, s_ref, qp, kp, vp, zp, abp, op):
    _load_padded(qp, q_ref)
    _load_padded(kp, k_ref)
    _load_padded(vp, v_ref)
    _load_padded(zp, z_ref)
    _load_padded(abp, ab_ref)
    nw = nw_ref[...]
    valid = lax.broadcasted_iota(jnp.int32, (S_C, LANES), 0) < S_T
    gb = jnp.where(valid, _gdn_gate_tile(abp[...], alog_ref[...], dtb_ref[...]), 0.0)
    d_all = _cumsum_rows(gb)
    row = lax.broadcasted_iota(jnp.int32, (S_C, 1), 0)
    heads = range(HEADS)
    cols = [slice(h * HEAD_DIM, (h + 1) * HEAD_DIM) for h in heads]
    qw, kw = qp[...], kp[...]
    qs = [qw[:, c] for c in cols]
    ks = [kw[:, c] for c in cols]
    ds = [d_all[:, h:h + 1] for h in heads]
    betas = [gb[:, HEADS + h:HEADS + h + 1] for h in heads]
    eds = [jnp.exp(d) for d in ds]
    Xs = [jnp.concatenate([vp[:, cols[h]] * betas[h], ks[h] * (betas[h] * eds[h])], axis=1)
          for h in heads]
    qkl = [[None] * S_T for _ in heads]
    for s in range(S_T):
        qk_w = qw * kw[s:s + 1, :]
        kk_w = kw * kw[s:s + 1, :]
        decay = [jnp.where(row >= s, jnp.exp(jnp.where(row >= s, ds[h] - ds[h][s:s + 1, :], 0.0)),
                           0.0) for h in heads]
        qk_s = [jnp.sum(qk_w[:, cols[h]], axis=-1, keepdims=True) for h in heads]
        for h in heads:
            qkl[h][s] = qk_s[h] * decay[h]
        if s < S_T - 1:
            kk_s = [jnp.sum(kk_w[:, cols[h]], axis=-1, keepdims=True) for h in heads]
            m_s = [jnp.where(row > s, betas[h] * kk_s[h] * decay[h], 0.0) for h in heads]
            Xs = [Xs[h] - m_s[h] * Xs[h][s:s + 1, :] for h in heads]
    d_last = [d[S_T - 1:S_T, :] for d in ds]
    qes = [qs[h] * eds[h] for h in heads]
    kdecs = [ks[h] * jnp.exp(d_last[h] - ds[h]) for h in heads]
    s0 = [s0_ref[0, h] for h in heads]
    sws = [_bdot(jnp.concatenate([Xs[h][:, HEAD_DIM:], qes[h]], axis=0), s0[h]) for h in heads]
    v_news = [Xs[h][:, :HEAD_DIM] - sws[h][:S_C] for h in heads]
    upds = [_bdot_tn(kdecs[h], v_news[h]) for h in heads]
    for h in heads:
        o = sws[h][S_C:]
        for s in range(S_T):
            o = o + qkl[h][s] * v_news[h][s:s + 1, :]
        s_ref[0, h] = jnp.exp(d_last[h]) * s0[h] + upds[h]
        op[:, cols[h]] = _head_norm(o, nw) * _silu(zp[:, cols[h]])
    o_ref[0] = op[0:S_T, :]


def _gdn_sample(qkv3, z3, ab3, alog, dtb, gdn_norm, s0, new_states, layer):
    nb = qkv3.shape[0]
    blk = lambda r: pl.BlockSpec((1, S_T, WIDTH), lambda b: (b, 0, r))
    st = pl.BlockSpec((None, 1, HEADS, HEAD_DIM, HEAD_DIM), lambda b: (layer, b, 0, 0, 0))
    return pl.pallas_call(
        _gdn_sample_body,
        grid=(nb,),
        in_specs=[blk(0), blk(1), blk(2), blk(0),
                  pl.BlockSpec((1, S_T, LANES), lambda b: (b, 0, 0)),
                  pl.BlockSpec((None, 1, LANES), lambda b: (layer, 0, 0)),
                  pl.BlockSpec((None, 1, LANES), lambda b: (layer, 0, 0)),
                  pl.BlockSpec((None, 1, HEAD_DIM), lambda b: (layer, 0, 0)),
                  st, pl.BlockSpec(memory_space=pl.ANY)],
        out_specs=[pl.BlockSpec((1, S_T, WIDTH), lambda b: (b, 0, 0)), st],
        out_shape=[jax.ShapeDtypeStruct((nb, S_T, WIDTH), F32),
                   jax.ShapeDtypeStruct(new_states.shape, F32)],
        input_output_aliases={9: 1},
        scratch_shapes=[pltpu.VMEM((S_C, WIDTH), F32)] * 4 + [pltpu.VMEM((S_C, LANES), F32),
                                                              pltpu.VMEM((S_C, WIDTH), F32)],
        compiler_params=_params(("parallel",)),
        name="gdn_sample",
    )(qkv3, qkv3, qkv3, z3, ab3, alog, dtb, gdn_norm, s0, new_states)


def _trunk(x, sample, states, W):
    rows = x.shape[0]
    tm = 512 if sample else 1024
    tm_ff = 512 if sample else 2048
    hg_states, gdn_states, conv_states = [], [], []
    if sample:
        hg_acc = jnp.zeros(states["hgrn"].shape, F32)
        gdn_acc = jnp.zeros(states["gdn"].shape, F32)
    for l in range(DEPTH):
        xn = _rmsnorm(x, W["ffn1_norm"][l], BF16)
        a = _swiglu_up(xn, W["ffn1_w_gu"], l, tm_ff)
        x = _down_residual(a, W["ffn1_w_down"], l, x, 0.5, 512, 512)

        xn = _rmsnorm(x, W["mix_norm"][l], BF16)
        proj = _matmul_nt(xn, W["w_in_t"], l, tm, 512, 0, MAIN_COLS)
        z = _matmul_nt(xn, W["w_z_t"], l, tm, 512, 0, WIDTH)
        ab = _matmul_nt(xn, W["w_in_t"], l, tm, LANES, MAIN_COLS // LANES, LANES)
        if sample:
            nb = rows // S_T
            proj3 = proj.reshape(nb, S_T, MAIN_COLS)
            ab3 = ab.reshape(nb, S_T, LANES)
            qkv3 = _conv_sample(proj3, states["conv"], W["conv_w"], l)
            o_hg, hg_acc = _hgrn_sample(proj3, W["lb"], W["hg_norm"], states["hgrn"], hg_acc, l)
            o_gdn, gdn_acc = _gdn_sample(qkv3, z.reshape(nb, S_T, WIDTH), ab3, W["alog"],
                                         W["dtb"], W["gdn_norm"], states["gdn"], gdn_acc, l)
            o_hg = o_hg.reshape(rows, WIDTH)
            o_gdn = o_gdn.reshape(rows, WIDTH)
            conv_new = proj3[:, S_T - (CONV_W - 1):, 4 * WIDTH:4 * WIDTH + CONV_DIM]
        else:
            batch, seq = states["batch"], states["seq"]
            o_hg, s_hg = _hgrn_prompt(proj, W["lb"], W["hg_norm"], l, batch, seq)
            o_gdn, s_gdn = _gdn_prompt(proj, W["conv_w"], z, ab, W["alog"], W["dtb"],
                                       W["gdn_norm"], l, batch, seq)
            conv_new = proj.reshape(batch, seq, MAIN_COLS)[
                :, seq - (CONV_W - 1):, 4 * WIDTH:4 * WIDTH + CONV_DIM]
        x = _mix_out_residual(o_hg, o_gdn, W["w_out"], l, x, tm, 512)

        xn = _rmsnorm(x, W["ffn2_norm"][l], BF16)
        a = _swiglu_up(xn, W["ffn2_w_gu"], l, tm_ff)
        x = _down_residual(a, W["ffn2_w_down"], l, x, 0.5, 512, 512)

        if not sample:
            hg_states.append(s_hg)
            gdn_states.append(s_gdn)
        conv_states.append(conv_new)
    y = _rmsnorm(x, W["final_norm"], F32)
    if not sample:
        hg_acc, gdn_acc = jnp.stack(hg_states), jnp.stack(gdn_states)
    return y, hg_acc, gdn_acc, jnp.stack(conv_states)


def kernel(x_prompt, x_sample, state_hgrn, state_gdn, state_conv, ffn1_norm, ffn1_w_gu, ffn1_w_down, mix_norm, w_in, hg_lower_bounds, hg_norm, gdn_conv_w, gdn_A_log, gdn_dt_bias, gdn_norm, w_out, ffn2_norm, ffn2_w_gu, ffn2_w_down, final_norm):
    batch, seq, _ = x_prompt.shape
    dec_batch, dec_seq, _ = x_sample.shape
    assert dec_seq == S_T and seq % H_TB == 0 and seq % G_TB == 0

    ab_lo = 4 * WIDTH + CONV_DIM
    w_in_t = jnp.swapaxes(w_in, 1, 2)
    row3 = lambda v: v.astype(F32)[:, None, :]
    pad_row = lambda v: row3(jnp.pad(v, ((0, 0), (0, LANES - v.shape[1]))))
    W = {
        "ffn1_norm": ffn1_norm, "mix_norm": mix_norm, "ffn2_norm": ffn2_norm,
        "final_norm": final_norm, "hg_norm": row3(hg_norm), "gdn_norm": row3(gdn_norm),
        "conv_w": gdn_conv_w,
        "ffn1_w_gu": ffn1_w_gu, "ffn1_w_down": ffn1_w_down.astype(BF16),
        "ffn2_w_gu": ffn2_w_gu, "ffn2_w_down": ffn2_w_down.astype(BF16),
        "w_out": w_out,
        "w_in_t": w_in_t,
        "w_z_t": w_in_t[:, ab_lo + 2 * HEADS:, :],
        "alog": pad_row(gdn_A_log), "dtb": pad_row(gdn_dt_bias),
        "lb": row3(_lower_bounds(hg_lower_bounds)),
    }

    y_p, hg_p, gdn_p, conv_p = _trunk(
        x_prompt.reshape(batch * seq, D_MODEL), False, {"batch": batch, "seq": seq}, W)
    y_s, hg_s, gdn_s, conv_s = _trunk(
        x_sample.reshape(dec_batch * dec_seq, D_MODEL), True,
        {"hgrn": state_hgrn, "gdn": state_gdn, "conv": state_conv}, W)
    return (y_p.reshape(batch, seq, D_MODEL), y_s.reshape(dec_batch, dec_seq, D_MODEL),
            hg_p, gdn_p, conv_p, hg_s, gdn_s, conv_s)
```

```python
import functools

import jax
import jax.numpy as jnp
from jax import lax
from jax.experimental import pallas as pl
from jax.experimental.pallas import tpu as pltpu

D_MODEL = 4096
DEPTH = 4
HEAD_DIM = 128
HEADS = 16
WIDTH = HEADS * HEAD_DIM
CONV_W = 4
CONV_DIM = 3 * WIDTH
D_FF = 11008
EPS = 1e-6
F_FLOOR = 1e-20
LOG2E = 1.4426950408889634
LANES = 128
SUBLANES = 8
FF_TILE = 256
CHUNK = 64
SUB = 16
MAIN_COLS = 4 * WIDTH + CONV_DIM
VMEM_LIMIT = 56 * 1024 * 1024

BF16 = jnp.bfloat16
F32 = jnp.float32


def _params(sem, vmem=VMEM_LIMIT):
    return pltpu.CompilerParams(dimension_semantics=sem, vmem_limit_bytes=vmem)


def _bdot(a, b):
    return jnp.dot(a.astype(BF16), b.astype(BF16), preferred_element_type=F32)


def _bdot_nt(a, b):
    return lax.dot_general(a.astype(BF16), b.astype(BF16), (((1,), (1,)), ((), ())),
                           preferred_element_type=F32)


def _bdot_tn(a, b):
    return lax.dot_general(a.astype(BF16), b.astype(BF16), (((0,), (0,)), ((), ())),
                           preferred_element_type=F32)


def _sigmoid(x):
    return 1.0 / (1.0 + jnp.exp(-x))


def _silu(x):
    return x * _sigmoid(x)


def _softplus(x):
    return jnp.maximum(x, 0.0) + jnp.log(1.0 + jnp.exp(-jnp.abs(x)))


def _cumsum_rows(x):
    n = x.shape[0]
    row = lax.broadcasted_iota(jnp.int32, x.shape, 0)
    s = 1
    while s < n:
        x = x + jnp.where(row >= s, pltpu.roll(x, s, axis=0), 0.0)
        s *= 2
    return x


def _row_to_col(r):
    n = r.shape[1]
    ri = lax.broadcasted_iota(jnp.int32, (n, n), 0)
    ci = lax.broadcasted_iota(jnp.int32, (n, n), 1)
    return jnp.sum(jnp.where(ri == ci, jnp.broadcast_to(r, (n, n)), 0.0), axis=1, keepdims=True)


def _col_to_row(c):
    n = c.shape[0]
    ri = lax.broadcasted_iota(jnp.int32, (n, n), 0)
    ci = lax.broadcasted_iota(jnp.int32, (n, n), 1)
    return jnp.sum(jnp.where(ri == ci, jnp.broadcast_to(c, (n, n)), 0.0), axis=0, keepdims=True)


def _rmsnorm_body(x_ref, w_ref, o_ref):
    x = x_ref[...]
    y = x * lax.rsqrt(jnp.mean(x * x, axis=-1, keepdims=True) + EPS)
    o_ref[...] = (y * w_ref[...]).astype(o_ref.dtype)


def _rmsnorm(x, w, out_dtype, tm=512):
    m = x.shape[0]
    return pl.pallas_call(
        _rmsnorm_body,
        grid=(m // tm,),
        in_specs=[pl.BlockSpec((tm, D_MODEL), lambda i: (i, 0)),
                  pl.BlockSpec((1, D_MODEL), lambda i: (0, 0))],
        out_specs=pl.BlockSpec((tm, D_MODEL), lambda i: (i, 0)),
        out_shape=jax.ShapeDtypeStruct((m, D_MODEL), out_dtype),
        compiler_params=_params(("parallel",)),
        name="rmsnorm",
    )(x, w.reshape(1, D_MODEL))


def _mm_nt_body(x_ref, w_ref, o_ref):
    o_ref[...] = lax.dot_general(x_ref[...], w_ref[...].astype(BF16), (((1,), (1,)), ((), ())),
                                 preferred_element_type=F32)


def _matmul_nt(xn, w_t, layer, tm, tn, blk0, n):
    m, k = xn.shape
    return pl.pallas_call(
        _mm_nt_body,
        grid=(m // tm, n // tn),
        in_specs=[pl.BlockSpec((tm, k), lambda i, j: (i, 0)),
                  pl.BlockSpec((None, tn, k), lambda i, j: (layer, blk0 + j, 0))],
        out_specs=pl.BlockSpec((tm, tn), lambda i, j: (i, j)),
        out_shape=jax.ShapeDtypeStruct((m, n), F32),
        compiler_params=_params(("parallel", "arbitrary")),
        name="matmul_nt",
    )(xn, w_t)


def _swiglu_body(x_ref, wg_ref, wu_ref, o_ref):
    w = jnp.concatenate([wg_ref[...].astype(BF16), wu_ref[...].astype(BF16)], axis=1)
    gu = jnp.dot(x_ref[...], w, preferred_element_type=F32)
    o_ref[...] = (_silu(gu[:, :FF_TILE]) * gu[:, FF_TILE:]).astype(o_ref.dtype)


def _swiglu_up(xn, w_gu, layer, tm):
    m, k = xn.shape
    nj = D_FF // FF_TILE
    return pl.pallas_call(
        _swiglu_body,
        grid=(m // tm, nj),
        in_specs=[pl.BlockSpec((tm, k), lambda i, j: (i, 0), pipeline_mode=pl.Buffered(1)),
                  pl.BlockSpec((None, k, FF_TILE), lambda i, j: (layer, 0, j)),
                  pl.BlockSpec((None, k, FF_TILE), lambda i, j: (layer, 0, j + nj))],
        out_specs=pl.BlockSpec((tm, FF_TILE), lambda i, j: (i, j)),
        out_shape=jax.ShapeDtypeStruct((m, D_FF), BF16),
        compiler_params=_params(("parallel", "arbitrary")),
        name="swiglu_up",
    )(xn, w_gu, w_gu)


def _down_body(scale, a_ref, w_ref, r_ref, o_ref):
    acc = jnp.dot(a_ref[...], w_ref[...], preferred_element_type=F32)
    o_ref[...] = r_ref[...] + scale * acc


def _down_residual(a, w, layer, res, scale, tm, tn):
    m, k = a.shape
    return pl.pallas_call(
        functools.partial(_down_body, scale),
        grid=(m // tm, D_MODEL // tn),
        in_specs=[pl.BlockSpec((tm, k), lambda i, j: (i, 0)),
                  pl.BlockSpec((None, k, tn), lambda i, j: (layer, 0, j)),
                  pl.BlockSpec((tm, tn), lambda i, j: (i, j))],
        out_specs=pl.BlockSpec((tm, tn), lambda i, j: (i, j)),
        out_shape=jax.ShapeDtypeStruct((m, D_MODEL), F32),
        compiler_params=_params(("parallel", "arbitrary")),
        name="down_residual",
    )(a, w, res)


def _mixout_body(a1_ref, a2_ref, w1_ref, w2_ref, r_ref, o_ref):
    acc = jnp.dot(a1_ref[...].astype(BF16), w1_ref[...].astype(BF16), preferred_element_type=F32)
    acc += jnp.dot(a2_ref[...].astype(BF16), w2_ref[...].astype(BF16), preferred_element_type=F32)
    o_ref[...] = r_ref[...] + acc


def _mix_out_residual(o_hg, o_gdn, w_out, layer, res, tm, tn):
    m = o_hg.shape[0]
    return pl.pallas_call(
        _mixout_body,
        grid=(m // tm, D_MODEL // tn),
        in_specs=[pl.BlockSpec((tm, WIDTH), lambda i, j: (i, 0)),
                  pl.BlockSpec((tm, WIDTH), lambda i, j: (i, 0)),
                  pl.BlockSpec((None, WIDTH, tn), lambda i, j: (layer, 0, j)),
                  pl.BlockSpec((None, WIDTH, tn), lambda i, j: (layer, 1, j)),
                  pl.BlockSpec((tm, tn), lambda i, j: (i, j))],
        out_specs=pl.BlockSpec((tm, tn), lambda i, j: (i, j)),
        out_shape=jax.ShapeDtypeStruct((m, D_MODEL), F32),
        compiler_params=_params(("parallel", "arbitrary")),
        name="mix_out_residual",
    )(o_hg, o_gdn, w_out, w_out, res)


def _lb_body(p_ref, o_ref):
    x = p_ref[...]
    e = jnp.exp(x - jnp.max(x, axis=0, keepdims=True))
    p = e / jnp.sum(e, axis=0, keepdims=True)
    rows = [p[0:1]]
    for l in range(1, DEPTH):
        rows.append(rows[-1] + p[l:l + 1])
    cs = jnp.concatenate(rows, axis=0)
    o_ref[...] = jnp.clip(cs - p[0:1], 0.0, 1.0)


def _lower_bounds(hg_lower_bounds):
    return pl.pallas_call(
        _lb_body,
        out_shape=jax.ShapeDtypeStruct((DEPTH, WIDTH), F32),
        name="lower_bounds",
    )(hg_lower_bounds)


def _hgrn_gates(zq, zf, lb, valid=None):
    q = _silu(zq)
    sig = _sigmoid(zf)
    f = lb + (1.0 - lb) * sig
    logf = jnp.log(jnp.maximum(f, F_FLOOR))
    k = (1.0 - lb) * (1.0 - sig)
    if valid is not None:
        logf = jnp.where(valid, logf, 0.0)
        k = jnp.where(valid, k, 0.0)
    return q, k, logf


def _gdn_wy(qs, ks, vs, d_cols, d_rows, beta_cols):
    c = qs[0].shape[0]
    n = range(len(qs))
    ri = lax.broadcasted_iota(jnp.int32, (c, c), 0)
    ci = lax.broadcasted_iota(jnp.int32, (c, c), 1)
    causal = ri >= ci
    strict = ri > ci
    Ls = [jnp.where(causal, jnp.exp(jnp.where(causal, d_cols[i] - d_rows[i], 0.0)), 0.0) for i in n]
    kks = [_bdot_nt(ks[i], ks[i]) for i in n]
    qks = [_bdot_nt(qs[i], ks[i]) for i in n]
    Ps = [jnp.where(strict, -(beta_cols[i] * kks[i] * Ls[i]), 0.0) for i in n]
    Tms = Ps
    span = 2
    while span < c:
        Ps = [_bdot(Ps[i], Ps[i]) for i in n]
        TPs = [_bdot(Tms[i], Ps[i]) for i in n]
        Tms = [Tms[i] + Ps[i] + TPs[i] for i in n]
        span *= 2
    eds = [jnp.exp(d_cols[i]) for i in n]
    rhss = [jnp.concatenate([vs[i] * beta_cols[i], ks[i] * (beta_cols[i] * eds[i])], axis=1)
            for i in n]
    Xs = [rhss[i] + _bdot(Tms[i], rhss[i]) for i in n]
    out = []
    for i in n:
        d_last = d_cols[i][c - 1:c, :]
        out.append(dict(u=Xs[i][:, :HEAD_DIM], w=Xs[i][:, HEAD_DIM:], qkl=qks[i] * Ls[i],
                        qe=qs[i] * eds[i], kdec=ks[i] * jnp.exp(d_last - d_cols[i]),
                        e_last=jnp.exp(d_last)))
    return out


def _gdn_apply(wy, Ss):
    n = range(len(wy))
    c = wy[0]["u"].shape[0]
    sws = [_bdot(jnp.concatenate([wy[i]["w"], wy[i]["qe"]], axis=0), Ss[i]) for i in n]
    v_news = [wy[i]["u"] - sws[i][:c] for i in n]
    o2s = [_bdot(wy[i]["qkl"], v_news[i]) for i in n]
    upds = [_bdot_tn(wy[i]["kdec"], v_news[i]) for i in n]
    outs = [sws[i][c:] + o2s[i] for i in n]
    return outs, [wy[i]["e_last"] * Ss[i] + upds[i] for i in n]


def _head_norm(o, w):
    return o * lax.rsqrt(jnp.mean(o * o, axis=-1, keepdims=True) + EPS) * w


def _gdn_gate_tile(ab, alog, dtb):
    lane = lax.broadcasted_iota(jnp.int32, ab.shape, 1)
    g = -jnp.exp(alog) * _softplus(ab + dtb)
    return jnp.where(lane < HEADS, g, _sigmoid(ab))


def _l2norm_heads(y, scale):
    outs = []
    for h in range(y.shape[-1] // HEAD_DIM):
        seg = y[:, h * HEAD_DIM:(h + 1) * HEAD_DIM]
        outs.append(seg * (lax.rsqrt(jnp.sum(seg * seg, axis=-1, keepdims=True) + EPS) * scale))
    return jnp.concatenate(outs, axis=-1)


H_TB = 128
G_TB = 128
G_HG = 8


def _causal_conv_silu(x_ref, w_ref, ext_ref):
    tb = x_ref.shape[0]
    ext_ref[SUBLANES:, :] = x_ref[...]
    y = x_ref[...] * w_ref[CONV_W - 1:CONV_W, :]
    for j in range(1, CONV_W):
        y = y + ext_ref[SUBLANES - j:SUBLANES - j + tb, :] * w_ref[CONV_W - 1 - j:CONV_W - j, :]
    ext_ref[0:SUBLANES, :] = x_ref[tb - SUBLANES:, :]
    return _silu(y)


def _hgrn_prompt_chunk(zq, zf, v, gate, lb, nw, s_scr, k_scr, b_scr):
    heads = range(HEADS)
    hc = [slice(h * HEAD_DIM, (h + 1) * HEAD_DIM) for h in heads]
    band = SUBLANES
    q, k, logf = _hgrn_gates(zq, zf, lb)
    b = _cumsum_rows(logf) * LOG2E
    k_scr[...] = k
    b_scr[...] = b
    bl = b[CHUNK - 1:CHUNK, :]
    qe = q * jnp.exp2(b)
    kdec = k * jnp.exp2(bl - b)
    e_last = jnp.exp2(bl)

    lane = lax.broadcasted_iota(jnp.int32, (band, CHUNK), 1)
    row = lax.broadcasted_iota(jnp.int32, (band, CHUNK), 0)
    a_bands = [[None] * (CHUNK // band) for _ in heads]
    for i in range(CHUNK // SUB):
        base = i * SUB
        q_lo, q_hi = q[base:base + band], q[base + band:base + SUB]
        b_lo, b_hi = b[base:base + band], b[base + band:base + SUB]
        if i == 0:
            a_lo = [jnp.zeros((band, CHUNK), F32) for _ in heads]
            a_hi = [jnp.zeros((band, CHUNK), F32) for _ in heads]
        else:
            ref = b[base - 1:base, :]
            qt = q[base:base + SUB] * jnp.exp2(b[base:base + SUB] - ref)
            kt = k * jnp.exp2(jnp.minimum(ref - b, 0.0))
            att = [jnp.where(lane[:1] < base, _bdot_nt(qt[:, hc[h]], kt[:, hc[h]]), 0.0)
                   for h in heads]
            a_lo = [att[h][:band] for h in heads]
            a_hi = [att[h][band:] for h in heads]
        for s in range(SUB):
            r = base + s
            k_s = k_scr[r:r + 1, :]
            b_s = b_scr[r:r + 1, :]
            p_hi = q_hi * k_s * jnp.exp2(jnp.minimum(b_hi - b_s, 0.0))
            if s < band:
                p_lo = q_lo * k_s * jnp.exp2(jnp.minimum(b_lo - b_s, 0.0))
                m_lo = (lane == r) & (row >= s)
                m_hi = lane == r
            else:
                m_hi = (lane == r) & (row >= s - band)
            for h in heads:
                if s < band:
                    a_lo[h] = jnp.where(m_lo, jnp.sum(p_lo[:, hc[h]], axis=-1, keepdims=True),
                                        a_lo[h])
                a_hi[h] = jnp.where(m_hi, jnp.sum(p_hi[:, hc[h]], axis=-1, keepdims=True),
                                    a_hi[h])
        for h in heads:
            a_bands[h][2 * i] = a_lo[h]
            a_bands[h][2 * i + 1] = a_hi[h]

    states = [s_scr[h] for h in heads]
    o_inter = [_bdot(qe[:, hc[h]], states[h]) for h in heads]
    o_intra = [_bdot(jnp.concatenate(a_bands[h], axis=0), v[:, hc[h]]) for h in heads]
    upd = [_bdot_tn(kdec[:, hc[h]], v[:, hc[h]]) for h in heads]
    outs = []
    for h in heads:
        s_scr[h] = _row_to_col(e_last[:, hc[h]]) * states[h] + upd[h]
        outs.append(_head_norm(o_inter[h] + o_intra[h], nw))
    return jnp.concatenate(outs, axis=1) * _sigmoid(gate)


def _hgrn_prompt_body(q_ref, f_ref, i_ref, gate_ref, lb_ref, nw_ref, o_ref, s_ref,
                      s_scr, k_scr, b_scr):
    t = pl.program_id(1)

    @pl.when(t == 0)
    def _():
        s_scr[...] = jnp.zeros_like(s_scr)

    lb = lb_ref[...]
    nw = nw_ref[...]
    for c in range(H_TB // CHUNK):
        rows = slice(c * CHUNK, (c + 1) * CHUNK)
        o = _hgrn_prompt_chunk(q_ref[rows, :], f_ref[rows, :], i_ref[rows, :], gate_ref[rows, :],
                               lb, nw, s_scr, k_scr, b_scr)
        o_ref[rows, :] = o.astype(o_ref.dtype)

    @pl.when(t == pl.num_programs(1) - 1)
    def _():
        s_ref[0] = s_scr[...]


def _hgrn_prompt(proj, lb, hg_norm, layer, batch, seq):
    nt = seq // H_TB
    blk = lambda r: pl.BlockSpec((H_TB, WIDTH), lambda b, t: (b * nt + t, r))
    return pl.pallas_call(
        _hgrn_prompt_body,
        grid=(batch, nt),
        in_specs=[blk(0), blk(1), blk(2), blk(3),
                  pl.BlockSpec((None, 1, WIDTH), lambda b, t: (layer, 0, 0)),
                  pl.BlockSpec((None, 1, HEAD_DIM), lambda b, t: (layer, 0, 0))],
        out_specs=[pl.BlockSpec((H_TB, WIDTH), lambda b, t: (b * nt + t, 0)),
                   pl.BlockSpec((1, HEADS, HEAD_DIM, HEAD_DIM), lambda b, t: (b, 0, 0, 0))],
        out_shape=[jax.ShapeDtypeStruct((batch * seq, WIDTH), BF16),
                   jax.ShapeDtypeStruct((batch, HEADS, HEAD_DIM, HEAD_DIM), F32)],
        scratch_shapes=[pltpu.VMEM((HEADS, HEAD_DIM, HEAD_DIM), F32),
                        pltpu.VMEM((CHUNK, WIDTH), F32),
                        pltpu.VMEM((CHUNK, WIDTH), F32)],
        compiler_params=_params(("parallel", "arbitrary")),
        name="hgrn_prompt",
    )(proj, proj, proj, proj, lb, hg_norm)


def _gdn_prompt_body(xq_ref, xk_ref, xv_ref, wq_ref, wk_ref, wv_ref, z_ref, ab_ref, alog_ref,
                     dtb_ref, nw_ref, o_ref, s_ref, s_scr, q_scr, k_scr, v_scr, cq, ck, cv):
    t = pl.program_id(1)

    @pl.when(t == 0)
    def _():
        s_scr[...] = jnp.zeros_like(s_scr)
        for ext in (cq, ck, cv):
            ext[0:SUBLANES, :] = jnp.zeros((SUBLANES, WIDTH), F32)

    q_scr[...] = _l2norm_heads(_causal_conv_silu(xq_ref, wq_ref, cq), HEAD_DIM ** -0.5)
    k_scr[...] = _l2norm_heads(_causal_conv_silu(xk_ref, wk_ref, ck), 1.0)
    v_scr[...] = _causal_conv_silu(xv_ref, wv_ref, cv)

    nw = nw_ref[...]
    gb = _gdn_gate_tile(ab_ref[...], alog_ref[...], dtb_ref[...])
    nc = G_TB // CHUNK
    d_all = jnp.concatenate(
        [_cumsum_rows(gb[c * CHUNK:(c + 1) * CHUNK]) for c in range(nc)], axis=0)
    d_t = d_all.T
    for c in range(nc):
        rows = slice(c * CHUNK, (c + 1) * CHUNK)
        for h0 in range(0, HEADS, G_HG):
            heads = list(range(h0, h0 + G_HG))
            cols = [slice(h * HEAD_DIM, (h + 1) * HEAD_DIM) for h in heads]
            wy = _gdn_wy([q_scr[rows, cs] for cs in cols],
                         [k_scr[rows, cs] for cs in cols],
                         [v_scr[rows, cs] for cs in cols],
                         [d_all[rows, h:h + 1] for h in heads],
                         [d_t[h:h + 1, rows] for h in heads],
                         [gb[rows, HEADS + h:HEADS + h + 1] for h in heads])
            outs, new_states = _gdn_apply(wy, [s_scr[h] for h in heads])
            for i, h in enumerate(heads):
                s_scr[h] = new_states[i]
                o = _head_norm(outs[i], nw) * _silu(z_ref[rows, cols[i]])
                o_ref[rows, cols[i]] = o.astype(o_ref.dtype)

    @pl.when(t == pl.num_programs(1) - 1)
    def _():
        s_ref[0] = s_scr[...]


def _gdn_prompt(proj, conv_w, z, ab, alog, dtb, gdn_norm, layer, batch, seq):
    nt = seq // G_TB
    qkv_blk = 4 * WIDTH // WIDTH
    blk = lambda r: pl.BlockSpec((G_TB, WIDTH), lambda b, t: (b * nt + t, r))
    cw = lambda c: pl.BlockSpec((None, CONV_W, WIDTH), lambda b, t: (layer, 0, c))
    row = lambda n: pl.BlockSpec((None, 1, n), lambda b, t: (layer, 0, 0))
    return pl.pallas_call(
        _gdn_prompt_body,
        grid=(batch, nt),
        in_specs=[blk(qkv_blk), blk(qkv_blk + 1), blk(qkv_blk + 2), cw(0), cw(1), cw(2), blk(0),
                  pl.BlockSpec((G_TB, LANES), lambda b, t: (b * nt + t, 0)),
                  row(LANES), row(LANES), row(HEAD_DIM)],
        out_specs=[pl.BlockSpec((G_TB, WIDTH), lambda b, t: (b * nt + t, 0)),
                   pl.BlockSpec((1, HEADS, HEAD_DIM, HEAD_DIM), lambda b, t: (b, 0, 0, 0))],
        out_shape=[jax.ShapeDtypeStruct((batch * seq, WIDTH), BF16),
                   jax.ShapeDtypeStruct((batch, HEADS, HEAD_DIM, HEAD_DIM), F32)],
        scratch_shapes=[pltpu.VMEM((HEADS, HEAD_DIM, HEAD_DIM), F32)]
        + [pltpu.VMEM((G_TB, WIDTH), F32)] * 3 + [pltpu.VMEM((SUBLANES + G_TB, WIDTH), F32)] * 3,
        compiler_params=_params(("parallel", "arbitrary")),
        name="gdn_prompt",
    )(proj, proj, proj, conv_w, conv_w, conv_w, z, ab, alog, dtb, gdn_norm)


S_BB = 8
S_T = 4
S_C = 8
S_HB = 2


def _conv_sample_body(x_ref, buf_ref, w_ref, o_ref):
    c = pl.program_id(1)
    xx = [buf_ref[:, j, :] for j in range(CONV_W - 1)] + [x_ref[:, t, :] for t in range(S_T)]
    for t in range(S_T):
        y = xx[t] * w_ref[0:1, :]
        for j in range(1, CONV_W):
            y = y + xx[t + j] * w_ref[j:j + 1, :]
        y = _silu(y)
        scale = jnp.where(c == 0, HEAD_DIM ** -0.5, 1.0)
        yn = _l2norm_heads(y, scale)
        o_ref[:, t, :] = jnp.where(c == 2, y, yn)


def _conv_sample(proj3, buf, conv_w, layer):
    nb = proj3.shape[0]
    qkv_blk = 4
    return pl.pallas_call(
        _conv_sample_body,
        grid=(nb // S_BB, 3),
        in_specs=[pl.BlockSpec((S_BB, S_T, WIDTH), lambda i, c: (i, 0, qkv_blk + c)),
                  pl.BlockSpec((None, S_BB, CONV_W - 1, WIDTH), lambda i, c: (layer, i, 0, c)),
                  pl.BlockSpec((None, CONV_W, WIDTH), lambda i, c: (layer, 0, c))],
        out_specs=pl.BlockSpec((S_BB, S_T, WIDTH), lambda i, c: (i, 0, c)),
        out_shape=jax.ShapeDtypeStruct((nb, S_T, CONV_DIM), F32),
        compiler_params=_params(("parallel", "arbitrary")),
        name="conv_sample",
    )(proj3, buf, conv_w)


def _load_padded(dst_ref, src_ref):
    dst_ref[...] = jnp.zeros_like(dst_ref)
    dst_ref[0:S_T, :] = src_ref[0]


def _hgrn_sample_body(q_ref, f_ref, i_ref, gate_ref, lb_ref, nw_ref, s0_ref, _aliased_states,
                      o_ref, s_ref, qp, fp, ip, gp, op):
    nw = nw_ref[...]
    heads = range(HEADS)
    cols = [slice(h * HEAD_DIM, (h + 1) * HEAD_DIM) for h in heads]
    valid = lax.broadcasted_iota(jnp.int32, (S_C, WIDTH), 0) < S_T
    row = lax.broadcasted_iota(jnp.int32, (S_C, 1), 0)
    for bb in range(S_HB):
        for dst, src in ((qp, q_ref), (fp, f_ref), (ip, i_ref), (gp, gate_ref)):
            dst[bb] = jnp.zeros((S_C, WIDTH), F32)
            dst[bb, 0:S_T, :] = src[bb]
        q, k, logf = _hgrn_gates(qp[bb], fp[bb], lb_ref[...], valid)
        v = ip[bb]
        b = _cumsum_rows(logf)
        bl = b[S_T - 1:S_T, :]
        qe = q * jnp.exp(b)
        kdec = k * jnp.exp(bl - b)
        e_last = jnp.exp(bl)
        o_diag = [jnp.zeros((S_C, HEAD_DIM), F32) for _ in heads]
        for s in range(S_T):
            p = q * k[s:s + 1, :] * jnp.exp(jnp.minimum(b - b[s:s + 1, :], 0.0))
            col = [jnp.where(row >= s, jnp.sum(p[:, cols[h]], axis=-1, keepdims=True), 0.0)
                   for h in heads]
            o_diag = [o_diag[h] + col[h] * v[s:s + 1, cols[h]] for h in heads]
        s0 = [s0_ref[bb, h] for h in heads]
        o_inter = [_bdot(qe[:, cols[h]], s0[h]) for h in heads]
        upd = [_bdot_tn(kdec[:, cols[h]], v[:, cols[h]]) for h in heads]
        for h in heads:
            s_ref[bb, h] = _row_to_col(e_last[:, cols[h]]) * s0[h] + upd[h]
            op[bb, :, cols[h]] = _head_norm(o_inter[h] + o_diag[h], nw)
        o_ref[bb] = (op[bb] * _sigmoid(gp[bb]))[0:S_T, :]


def _hgrn_sample(proj3, lb, hg_norm, s0, new_states, layer):
    nb = proj3.shape[0]
    blk = lambda r: pl.BlockSpec((S_HB, S_T, WIDTH), lambda b: (b, 0, r))
    st = pl.BlockSpec((None, S_HB, HEADS, HEAD_DIM, HEAD_DIM), lambda b: (layer, b, 0, 0, 0))
    return pl.pallas_call(
        _hgrn_sample_body,
        grid=(nb // S_HB,),
        in_specs=[blk(0), blk(1), blk(2), blk(3),
                  pl.BlockSpec((None, 1, WIDTH), lambda b: (layer, 0, 0)),
                  pl.BlockSpec((None, 1, HEAD_DIM), lambda b: (layer, 0, 0)),
                  st, pl.BlockSpec(memory_space=pl.ANY)],
        out_specs=[pl.BlockSpec((S_HB, S_T, WIDTH), lambda b: (b, 0, 0)), st],
        out_shape=[jax.ShapeDtypeStruct((nb, S_T, WIDTH), F32),
                   jax.ShapeDtypeStruct(new_states.shape, F32)],
        input_output_aliases={7: 1},
        scratch_shapes=[pltpu.VMEM((S_HB, S_C, WIDTH), F32)] * 5,
        compiler_params=_params(("parallel",)),
        name="hgrn_sample",
    )(proj3, proj3, proj3, proj3, lb, hg_norm, s0, new_states)


def _gdn_sample_body(q_ref, k_ref, v_ref, z_ref, ab_ref, alog_ref, dtb_ref, nw_ref, s0_ref,
                     _aliased_states, o_ref, s_ref, qp, kp, vp, zp, abp, op):
    _load_padded(qp, q_ref)
    _load_padded(kp, k_ref)
    _load_padded(vp, v_ref)
    _load_padded(zp, z_ref)
    _load_padded(abp, ab_ref)
    nw = nw_ref[...]
    valid = lax.broadcasted_iota(jnp.int32, (S_C, LANES), 0) < S_T
    gb = jnp.where(valid, _gdn_gate_tile(abp[...], alog_ref[...], dtb_ref[...]), 0.0)
    d_all = _cumsum_rows(gb)
    row = lax.broadcasted_iota(jnp.int32, (S_C, 1), 0)
    heads = range(HEADS)
    cols = [slice(h * HEAD_DIM, (h + 1) * HEAD_DIM) for h in heads]
    qw, kw = qp[...], kp[...]
    qs = [qw[:, c] for c in cols]
    ks = [kw[:, c] for c in cols]
    ds = [d_all[:, h:h + 1] for h in heads]
    betas = [gb[:, HEADS + h:HEADS + h + 1] for h in heads]
    eds = [jnp.exp(d) for d in ds]
    Xs = [jnp.concatenate([vp[:, cols[h]] * betas[h], ks[h] * (betas[h] * eds[h])], axis=1)
          for h in heads]
    qkl = [[None] * S_T for _ in heads]
    for s in range(S_T):
        qk_w = qw * kw[s:s + 1, :]
        kk_w = kw * kw[s:s + 1, :]
        decay = [jnp.where(row >= s, jnp.exp(jnp.where(row >= s, ds[h] - ds[h][s:s + 1, :], 0.0)),
                           0.0) for h in heads]
        qk_s = [jnp.sum(qk_w[:, cols[h]], axis=-1, keepdims=True) for h in heads]
        for h in heads:
            qkl[h][s] = qk_s[h] * decay[h]
        if s < S_T - 1:
            kk_s = [jnp.sum(kk_w[:, cols[h]], axis=-1, keepdims=True) for h in heads]
            m_s = [jnp.where(row > s, betas[h] * kk_s[h] * decay[h], 0.0) for h in heads]
            Xs = [Xs[h] - m_s[h] * Xs[h][s:s + 1, :] for h in heads]
    d_last = [d[S_T - 1:S_T, :] for d in ds]
    qes = [qs[h] * eds[h] for h in heads]
    kdecs = [ks[h] * jnp.exp(d_last[h] - ds[h]) for h in heads]
    s0 = [s0_ref[0, h] for h in heads]
    sws = [_bdot(jnp.concatenate([Xs[h][:, HEAD_DIM:], qes[h]], axis=0), s0[h]) for h in heads]
    v_news = [Xs[h][:, :HEAD_DIM] - sws[h][:S_C] for h in heads]
    upds = [_bdot_tn(kdecs[h], v_news[h]) for h in heads]
    for h in heads:
        o = sws[h][S_C:]
        for s in range(S_T):
            o = o + qkl[h][s] * v_news[h][s:s + 1, :]
        s_ref[0, h] = jnp.exp(d_last[h]) * s0[h] + upds[h]
        op[:, cols[h]] = _head_norm(o, nw) * _silu(zp[:, cols[h]])
    o_ref[0] = op[0:S_T, :]


def _gdn_sample(qkv3, z3, ab3, alog, dtb, gdn_norm, s0, new_states, layer):
    nb = qkv3.shape[0]
    blk = lambda r: pl.BlockSpec((1, S_T, WIDTH), lambda b: (b, 0, r))
    st = pl.BlockSpec((None, 1, HEADS, HEAD_DIM, HEAD_DIM), lambda b: (layer, b, 0, 0, 0))
    return pl.pallas_call(
        _gdn_sample_body,
        grid=(nb,),
        in_specs=[blk(0), blk(1), blk(2), blk(0),
                  pl.BlockSpec((1, S_T, LANES), lambda b: (b, 0, 0)),
                  pl.BlockSpec((None, 1, LANES), lambda b: (layer, 0, 0)),
                  pl.BlockSpec((None, 1, LANES), lambda b: (layer, 0, 0)),
                  pl.BlockSpec((None, 1, HEAD_DIM), lambda b: (layer, 0, 0)),
                  st, pl.BlockSpec(memory_space=pl.ANY)],
        out_specs=[pl.BlockSpec((1, S_T, WIDTH), lambda b: (b, 0, 0)), st],
        out_shape=[jax.ShapeDtypeStruct((nb, S_T, WIDTH), F32),
                   jax.ShapeDtypeStruct(new_states.shape, F32)],
        input_output_aliases={9: 1},
        scratch_shapes=[pltpu.VMEM((S_C, WIDTH), F32)] * 4 + [pltpu.VMEM((S_C, LANES), F32),
                                                              pltpu.VMEM((S_C, WIDTH), F32)],
        compiler_params=_params(("parallel",)),
        name="gdn_sample",
    )(qkv3, qkv3, qkv3, z3, ab3, alog, dtb, gdn_norm, s0, new_states)


def _trunk(x, sample, states, W):
    rows = x.shape[0]
    tm = 512 if sample else 1024
    tm_ff = 512 if sample else 2048
    hg_states, gdn_states, conv_states = [], [], []
    if sample:
        hg_acc = jnp.zeros(states["hgrn"].shape, F32)
        gdn_acc = jnp.zeros(states["gdn"].shape, F32)
    for l in range(DEPTH):
        xn = _rmsnorm(x, W["ffn1_norm"][l], BF16)
        a = _swiglu_up(xn, W["ffn1_w_gu"], l, tm_ff)
        x = _down_residual(a, W["ffn1_w_down"], l, x, 0.5, 512, 512)

        xn = _rmsnorm(x, W["mix_norm"][l], BF16)
        proj = _matmul_nt(xn, W["w_in_t"], l, tm, 512, 0, MAIN_COLS)
        z = _matmul_nt(xn, W["w_z_t"], l, tm, 512, 0, WIDTH)
        ab = _matmul_nt(xn, W["w_in_t"], l, tm, LANES, MAIN_COLS // LANES, LANES)
        if sample:
            nb = rows // S_T
            proj3 = proj.reshape(nb, S_T, MAIN_COLS)
            ab3 = ab.reshape(nb, S_T, LANES)
            qkv3 = _conv_sample(proj3, states["conv"], W["conv_w"], l)
            o_hg, hg_acc = _hgrn_sample(proj3, W["lb"], W["hg_norm"], states["hgrn"], hg_acc, l)
            o_gdn, gdn_acc = _gdn_sample(qkv3, z.reshape(nb, S_T, WIDTH), ab3, W["alog"],
                                         W["dtb"], W["gdn_norm"], states["gdn"], gdn_acc, l)
            o_hg = o_hg.reshape(rows, WIDTH)
            o_gdn = o_gdn.reshape(rows, WIDTH)
            conv_new = proj3[:, S_T - (CONV_W - 1):, 4 * WIDTH:4 * WIDTH + CONV_DIM]
        else:
            batch, seq = states["batch"], states["seq"]
            o_hg, s_hg = _hgrn_prompt(proj, W["lb"], W["hg_norm"], l, batch, seq)
            o_gdn, s_gdn = _gdn_prompt(proj, W["conv_w"], z, ab, W["alog"], W["dtb"],
                                       W["gdn_norm"], l, batch, seq)
            conv_new = proj.reshape(batch, seq, MAIN_COLS)[
                :, seq - (CONV_W - 1):, 4 * WIDTH:4 * WIDTH + CONV_DIM]
        x = _mix_out_residual(o_hg, o_gdn, W["w_out"], l, x, tm, 512)

        xn = _rmsnorm(x, W["ffn2_norm"][l], BF16)
        a = _swiglu_up(xn, W["ffn2_w_gu"], l, tm_ff)
        x = _down_residual(a, W["ffn2_w_down"], l, x, 0.5, 512, 512)

        if not sample:
            hg_states.append(s_hg)
            gdn_states.append(s_gdn)
        conv_states.append(conv_new)
    y = _rmsnorm(x, W["final_norm"], F32)
    if not sample:
        hg_acc, gdn_acc = jnp.stack(hg_states), jnp.stack(gdn_states)
    return y, hg_acc, gdn_acc, jnp.stack(conv_states)


def kernel(x_prompt, x_sample, state_hgrn, state_gdn, state_conv, ffn1_norm, ffn1_w_gu, ffn1_w_down, mix_norm, w_in, hg_lower_bounds, hg_norm, gdn_conv_w, gdn_A_log, gdn_dt_bias, gdn_norm, w_out, ffn2_norm, ffn2_w_gu, ffn2_w_down, final_norm):
    batch, seq, _ = x_prompt.shape
    dec_batch, dec_seq, _ = x_sample.shape
    assert dec_seq == S_T and seq % H_TB == 0 and seq % G_TB == 0

    ab_lo = 4 * WIDTH + CONV_DIM
    w_in_t = jnp.swapaxes(w_in, 1, 2)
    row3 = lambda v: v.astype(F32)[:, None, :]
    pad_row = lambda v: row3(jnp.pad(v, ((0, 0), (0, LANES - v.shape[1]))))
    W = {
        "ffn1_norm": ffn1_norm, "mix_norm": mix_norm, "ffn2_norm": ffn2_norm,
        "final_norm": final_norm, "hg_norm": row3(hg_norm), "gdn_norm": row3(gdn_norm),
        "conv_w": gdn_conv_w,
        "ffn1_w_gu": ffn1_w_gu, "ffn1_w_down": ffn1_w_down.astype(BF16),
        "ffn2_w_gu": ffn2_w_gu, "ffn2_w_down": ffn2_w_down.astype(BF16),
        "w_out": w_out,
        "w_in_t": w_in_t,
        "w_z_t": w_in_t[:, ab_lo + 2 * HEADS:, :],
        "alog": pad_row(gdn_A_log), "dtb": pad_row(gdn_dt_bias),
        "lb": row3(_lower_bounds(hg_lower_bounds)),
    }

    y_p, hg_p, gdn_p, conv_p = _trunk(
        x_prompt.reshape(batch * seq, D_MODEL), False, {"batch": batch, "seq": seq}, W)
    y_s, hg_s, gdn_s, conv_s = _trunk(
        x_sample.reshape(dec_batch * dec_seq, D_MODEL), True,
        {"hgrn": state_hgrn, "gdn": state_gdn, "conv": state_conv}, W)
    return (y_p.reshape(batch, seq, D_MODEL), y_s.reshape(dec_batch, dec_seq, D_MODEL),
            hg_p, gdn_p, conv_p, hg_s, gdn_s, conv_s)
```

```python
import functools

import jax
import jax.numpy as jnp
from jax import lax
from jax.experimental import pallas as pl
from jax.experimental.pallas import tpu as pltpu

D_MODEL = 4096
DEPTH = 4
HEAD_DIM = 128
HEADS = 16
WIDTH = HEADS * HEAD_DIM
CONV_W = 4
CONV_DIM = 3 * WIDTH
D_FF = 11008
EPS = 1e-6
F_FLOOR = 1e-20
LOG2E = 1.4426950408889634
LANES = 128
SUBLANES = 8
FF_TILE = 256
CHUNK = 64
SUB = 16
MAIN_COLS = 4 * WIDTH + CONV_DIM
VMEM_LIMIT = 56 * 1024 * 1024

BF16 = jnp.bfloat16
F32 = jnp.float32


def _params(sem, vmem=VMEM_LIMIT):
    return pltpu.CompilerParams(dimension_semantics=sem, vmem_limit_bytes=vmem)


def _bdot(a, b):
    return jnp.dot(a.astype(BF16), b.astype(BF16), preferred_element_type=F32)


def _bdot_nt(a, b):
    return lax.dot_general(a.astype(BF16), b.astype(BF16), (((1,), (1,)), ((), ())),
                           preferred_element_type=F32)


def _bdot_tn(a, b):
    return lax.dot_general(a.astype(BF16), b.astype(BF16), (((0,), (0,)), ((), ())),
                           preferred_element_type=F32)


def _sigmoid(x):
    return 1.0 / (1.0 + jnp.exp(-x))


def _silu(x):
    return x * _sigmoid(x)


def _softplus(x):
    return jnp.maximum(x, 0.0) + jnp.log(1.0 + jnp.exp(-jnp.abs(x)))


def _cumsum_rows(x):
    n = x.shape[0]
    row = lax.broadcasted_iota(jnp.int32, x.shape, 0)
    s = 1
    while s < n:
        x = x + jnp.where(row >= s, pltpu.roll(x, s, axis=0), 0.0)
        s *= 2
    return x


def _cumsum_rows_mxu(x):
    n = x.shape[0]
    ri = lax.broadcasted_iota(jnp.int32, (n, n), 0)
    ci = lax.broadcasted_iota(jnp.int32, (n, n), 1)
    tri = jnp.where(ri >= ci, 1.0, 0.0).astype(BF16)
    hi = x.astype(BF16)
    r1 = x - hi.astype(F32)
    mid = r1.astype(BF16)
    lo = (r1 - mid.astype(F32)).astype(BF16)
    dot = lambda p: jnp.dot(tri, p, preferred_element_type=F32)
    return dot(hi) + dot(mid) + dot(lo)


def _row_to_col(r):
    n = r.shape[1]
    ri = lax.broadcasted_iota(jnp.int32, (n, n), 0)
    ci = lax.broadcasted_iota(jnp.int32, (n, n), 1)
    return jnp.sum(jnp.where(ri == ci, jnp.broadcast_to(r, (n, n)), 0.0), axis=1, keepdims=True)


def _col_to_row(c):
    n = c.shape[0]
    ri = lax.broadcasted_iota(jnp.int32, (n, n), 0)
    ci = lax.broadcasted_iota(jnp.int32, (n, n), 1)
    return jnp.sum(jnp.where(ri == ci, jnp.broadcast_to(c, (n, n)), 0.0), axis=0, keepdims=True)


def _rmsnorm_body(x_ref, w_ref, o_ref):
    x = x_ref[...]
    y = x * lax.rsqrt(jnp.mean(x * x, axis=-1, keepdims=True) + EPS)
    o_ref[...] = (y * w_ref[...]).astype(o_ref.dtype)


def _rmsnorm(x, w, out_dtype, tm=512):
    m = x.shape[0]
    return pl.pallas_call(
        _rmsnorm_body,
        grid=(m // tm,),
        in_specs=[pl.BlockSpec((tm, D_MODEL), lambda i: (i, 0)),
                  pl.BlockSpec((1, D_MODEL), lambda i: (0, 0))],
        out_specs=pl.BlockSpec((tm, D_MODEL), lambda i: (i, 0)),
        out_shape=jax.ShapeDtypeStruct((m, D_MODEL), out_dtype),
        compiler_params=_params(("parallel",)),
        name="rmsnorm",
    )(x, w.reshape(1, D_MODEL))


def _mm_nt_body(x_ref, w_ref, o_ref):
    o_ref[...] = lax.dot_general(x_ref[...], w_ref[...].astype(BF16), (((1,), (1,)), ((), ())),
                                 preferred_element_type=F32)


def _matmul_nt(xn, w_t, layer, tm, tn, blk0, n):
    m, k = xn.shape
    return pl.pallas_call(
        _mm_nt_body,
        grid=(m // tm, n // tn),
        in_specs=[pl.BlockSpec((tm, k), lambda i, j: (i, 0)),
                  pl.BlockSpec((None, tn, k), lambda i, j: (layer, blk0 + j, 0))],
        out_specs=pl.BlockSpec((tm, tn), lambda i, j: (i, j)),
        out_shape=jax.ShapeDtypeStruct((m, n), F32),
        compiler_params=_params(("parallel", "arbitrary")),
        name="matmul_nt",
    )(xn, w_t)


def _swiglu_body(x_ref, wg_ref, wu_ref, o_ref):
    w = jnp.concatenate([wg_ref[...].astype(BF16), wu_ref[...].astype(BF16)], axis=1)
    gu = jnp.dot(x_ref[...], w, preferred_element_type=F32)
    o_ref[...] = (_silu(gu[:, :FF_TILE]) * gu[:, FF_TILE:]).astype(o_ref.dtype)


def _swiglu_up(xn, w_gu, layer, tm):
    m, k = xn.shape
    nj = D_FF // FF_TILE
    return pl.pallas_call(
        _swiglu_body,
        grid=(m // tm, nj),
        in_specs=[pl.BlockSpec((tm, k), lambda i, j: (i, 0), pipeline_mode=pl.Buffered(1)),
                  pl.BlockSpec((None, k, FF_TILE), lambda i, j: (layer, 0, j)),
                  pl.BlockSpec((None, k, FF_TILE), lambda i, j: (layer, 0, j + nj))],
        out_specs=pl.BlockSpec((tm, FF_TILE), lambda i, j: (i, j)),
        out_shape=jax.ShapeDtypeStruct((m, D_FF), BF16),
        compiler_params=_params(("parallel", "arbitrary")),
        name="swiglu_up",
    )(xn, w_gu, w_gu)


def _down_body(scale, a_ref, w_ref, r_ref, o_ref):
    acc = jnp.dot(a_ref[...], w_ref[...], preferred_element_type=F32)
    o_ref[...] = r_ref[...] + scale * acc


def _down_residual(a, w, layer, res, scale, tm, tn):
    m, k = a.shape
    return pl.pallas_call(
        functools.partial(_down_body, scale),
        grid=(m // tm, D_MODEL // tn),
        in_specs=[pl.BlockSpec((tm, k), lambda i, j: (i, 0)),
                  pl.BlockSpec((None, k, tn), lambda i, j: (layer, 0, j)),
                  pl.BlockSpec((tm, tn), lambda i, j: (i, j))],
        out_specs=pl.BlockSpec((tm, tn), lambda i, j: (i, j)),
        out_shape=jax.ShapeDtypeStruct((m, D_MODEL), F32),
        compiler_params=_params(("parallel", "arbitrary")),
        name="down_residual",
    )(a, w, res)


def _mixout_body(a1_ref, a2_ref, w1_ref, w2_ref, r_ref, o_ref):
    acc = jnp.dot(a1_ref[...].astype(BF16), w1_ref[...].astype(BF16), preferred_element_type=F32)
    acc += jnp.dot(a2_ref[...].astype(BF16), w2_ref[...].astype(BF16), preferred_element_type=F32)
    o_ref[...] = r_ref[...] + acc


def _mix_out_residual(o_hg, o_gdn, w_out, layer, res, tm, tn):
    m = o_hg.shape[0]
    return pl.pallas_call(
        _mixout_body,
        grid=(m // tm, D_MODEL // tn),
        in_specs=[pl.BlockSpec((tm, WIDTH), lambda i, j: (i, 0)),
                  pl.BlockSpec((tm, WIDTH), lambda i, j: (i, 0)),
                  pl.BlockSpec((None, WIDTH, tn), lambda i, j: (layer, 0, j)),
                  pl.BlockSpec((None, WIDTH, tn), lambda i, j: (layer, 1, j)),
                  pl.BlockSpec((tm, tn), lambda i, j: (i, j))],
        out_specs=pl.BlockSpec((tm, tn), lambda i, j: (i, j)),
        out_shape=jax.ShapeDtypeStruct((m, D_MODEL), F32),
        compiler_params=_params(("parallel", "arbitrary")),
        name="mix_out_residual",
    )(o_hg, o_gdn, w_out, w_out, res)


def _lb_body(p_ref, o_ref):
    x = p_ref[...]
    e = jnp.exp(x - jnp.max(x, axis=0, keepdims=True))
    p = e / jnp.sum(e, axis=0, keepdims=True)
    rows = [p[0:1]]
    for l in range(1, DEPTH):
        rows.append(rows[-1] + p[l:l + 1])
    cs = jnp.concatenate(rows, axis=0)
    o_ref[...] = jnp.clip(cs - p[0:1], 0.0, 1.0)


def _lower_bounds(hg_lower_bounds):
    return pl.pallas_call(
        _lb_body,
        out_shape=jax.ShapeDtypeStruct((DEPTH, WIDTH), F32),
        name="lower_bounds",
    )(hg_lower_bounds)


def _hgrn_gates(zq, zf, lb, valid=None):
    q = _silu(zq)
    sig = _sigmoid(zf)
    f = lb + (1.0 - lb) * sig
    logf = jnp.log(jnp.maximum(f, F_FLOOR))
    k = (1.0 - lb) * (1.0 - sig)
    if valid is not None:
        logf = jnp.where(valid, logf, 0.0)
        k = jnp.where(valid, k, 0.0)
    return q, k, logf


def _gdn_wy(qs, ks, vs, d_cols, d_rows, beta_cols):
    c = qs[0].shape[0]
    n = range(len(qs))
    ri = lax.broadcasted_iota(jnp.int32, (c, c), 0)
    ci = lax.broadcasted_iota(jnp.int32, (c, c), 1)
    causal = ri >= ci
    strict = ri > ci
    Ls = [jnp.where(causal, jnp.exp(jnp.where(causal, d_cols[i] - d_rows[i], 0.0)), 0.0) for i in n]
    kks = [_bdot_nt(ks[i], ks[i]) for i in n]
    qks = [_bdot_nt(qs[i], ks[i]) for i in n]
    Ps = [jnp.where(strict, -(beta_cols[i] * kks[i] * Ls[i]), 0.0) for i in n]
    Tms = Ps
    span = 2
    while span < c:
        Ps = [_bdot(Ps[i], Ps[i]) for i in n]
        TPs = [_bdot(Tms[i], Ps[i]) for i in n]
        Tms = [Tms[i] + Ps[i] + TPs[i] for i in n]
        span *= 2
    eds = [jnp.exp(d_cols[i]) for i in n]
    rhss = [jnp.concatenate([vs[i] * beta_cols[i], ks[i] * (beta_cols[i] * eds[i])], axis=1)
            for i in n]
    Xs = [rhss[i] + _bdot(Tms[i], rhss[i]) for i in n]
    out = []
    for i in n:
        d_last = d_cols[i][c - 1:c, :]
        out.append(dict(u=Xs[i][:, :HEAD_DIM], w=Xs[i][:, HEAD_DIM:], qkl=qks[i] * Ls[i],
                        qe=qs[i] * eds[i], kdec=ks[i] * jnp.exp(d_last - d_cols[i]),
                        e_last=jnp.exp(d_last)))
    return out


def _gdn_apply(wy, Ss):
    n = range(len(wy))
    c = wy[0]["u"].shape[0]
    sws = [_bdot(jnp.concatenate([wy[i]["w"], wy[i]["qe"]], axis=0), Ss[i]) for i in n]
    v_news = [wy[i]["u"] - sws[i][:c] for i in n]
    o2s = [_bdot(wy[i]["qkl"], v_news[i]) for i in n]
    upds = [_bdot_tn(wy[i]["kdec"], v_news[i]) for i in n]
    outs = [sws[i][c:] + o2s[i] for i in n]
    return outs, [wy[i]["e_last"] * Ss[i] + upds[i] for i in n]


def _head_norm(o, w):
    return o * lax.rsqrt(jnp.mean(o * o, axis=-1, keepdims=True) + EPS) * w


def _gdn_gate_tile(ab, alog, dtb):
    lane = lax.broadcasted_iota(jnp.int32, ab.shape, 1)
    g = -jnp.exp(alog) * _softplus(ab + dtb)
    return jnp.where(lane < HEADS, g, _sigmoid(ab))


def _l2norm_heads(y, scale):
    outs = []
    for h in range(y.shape[-1] // HEAD_DIM):
        seg = y[:, h * HEAD_DIM:(h + 1) * HEAD_DIM]
        outs.append(seg * (lax.rsqrt(jnp.sum(seg * seg, axis=-1, keepdims=True) + EPS) * scale))
    return jnp.concatenate(outs, axis=-1)


H_TB = 128
G_TB = 128
G_HG = 8


def _causal_conv_silu(x_ref, w_ref, ext_ref):
    tb = x_ref.shape[0]
    ext_ref[SUBLANES:, :] = x_ref[...]
    y = x_ref[...] * w_ref[CONV_W - 1:CONV_W, :]
    for j in range(1, CONV_W):
        y = y + ext_ref[SUBLANES - j:SUBLANES - j + tb, :] * w_ref[CONV_W - 1 - j:CONV_W - j, :]
    ext_ref[0:SUBLANES, :] = x_ref[tb - SUBLANES:, :]
    return _silu(y)


def _hgrn_prompt_chunk(zq, zf, v, gate, lb, nw, s_scr, k_scr, b_scr):
    heads = range(HEADS)
    hc = [slice(h * HEAD_DIM, (h + 1) * HEAD_DIM) for h in heads]
    band = SUBLANES
    q, k, logf = _hgrn_gates(zq, zf, lb)
    b = _cumsum_rows_mxu(logf) * LOG2E
    k_scr[...] = k
    b_scr[...] = b
    bl = b[CHUNK - 1:CHUNK, :]
    qe = q * jnp.exp2(b)
    kdec = k * jnp.exp2(bl - b)
    e_last = jnp.exp2(bl)

    lane = lax.broadcasted_iota(jnp.int32, (band, CHUNK), 1)
    row = lax.broadcasted_iota(jnp.int32, (band, CHUNK), 0)
    a_bands = [[None] * (CHUNK // band) for _ in heads]
    for i in range(CHUNK // SUB):
        base = i * SUB
        q_lo, q_hi = q[base:base + band], q[base + band:base + SUB]
        b_lo, b_hi = b[base:base + band], b[base + band:base + SUB]
        if i == 0:
            a_lo = [jnp.zeros((band, CHUNK), F32) for _ in heads]
            a_hi = [jnp.zeros((band, CHUNK), F32) for _ in heads]
        else:
            ref = b[base - 1:base, :]
            qt = q[base:base + SUB] * jnp.exp2(b[base:base + SUB] - ref)
            kt = k * jnp.exp2(jnp.minimum(ref - b, 0.0))
            att = [jnp.where(lane[:1] < base, _bdot_nt(qt[:, hc[h]], kt[:, hc[h]]), 0.0)
                   for h in heads]
            a_lo = [att[h][:band] for h in heads]
            a_hi = [att[h][band:] for h in heads]
        for s in range(SUB):
            r = base + s
            k_s = k_scr[r:r + 1, :]
            b_s = b_scr[r:r + 1, :]
            p_hi = q_hi * k_s * jnp.exp2(jnp.minimum(b_hi - b_s, 0.0))
            if s < band:
                p_lo = q_lo * k_s * jnp.exp2(jnp.minimum(b_lo - b_s, 0.0))
                m_lo = (lane == r) & (row >= s)
                m_hi = lane == r
            else:
                m_hi = (lane == r) & (row >= s - band)
            for h in heads:
                if s < band:
                    a_lo[h] = jnp.where(m_lo, jnp.sum(p_lo[:, hc[h]], axis=-1, keepdims=True),
                                        a_lo[h])
                a_hi[h] = jnp.where(m_hi, jnp.sum(p_hi[:, hc[h]], axis=-1, keepdims=True),
                                    a_hi[h])
        for h in heads:
            a_bands[h][2 * i] = a_lo[h]
            a_bands[h][2 * i + 1] = a_hi[h]

    states = [s_scr[h] for h in heads]
    o_inter = [_bdot(qe[:, hc[h]], states[h]) for h in heads]
    o_intra = [_bdot(jnp.concatenate(a_bands[h], axis=0), v[:, hc[h]]) for h in heads]
    upd = [_bdot_tn(kdec[:, hc[h]], v[:, hc[h]]) for h in heads]
    outs = []
    for h in heads:
        s_scr[h] = _row_to_col(e_last[:, hc[h]]) * states[h] + upd[h]
        outs.append(_head_norm(o_inter[h] + o_intra[h], nw))
    return jnp.concatenate(outs, axis=1) * _sigmoid(gate)


def _hgrn_prompt_body(q_ref, f_ref, i_ref, gate_ref, lb_ref, nw_ref, o_ref, s_ref,
                      s_scr, k_scr, b_scr):
    t = pl.program_id(1)

    @pl.when(t == 0)
    def _():
        s_scr[...] = jnp.zeros_like(s_scr)

    lb = lb_ref[...]
    nw = nw_ref[...]
    for c in range(H_TB // CHUNK):
        rows = slice(c * CHUNK, (c + 1) * CHUNK)
        o = _hgrn_prompt_chunk(q_ref[rows, :], f_ref[rows, :], i_ref[rows, :], gate_ref[rows, :],
                               lb, nw, s_scr, k_scr, b_scr)
        o_ref[rows, :] = o.astype(o_ref.dtype)

    @pl.when(t == pl.num_programs(1) - 1)
    def _():
        s_ref[0] = s_scr[...]


def _hgrn_prompt(proj, lb, hg_norm, layer, batch, seq):
    nt = seq // H_TB
    blk = lambda r: pl.BlockSpec((H_TB, WIDTH), lambda b, t: (b * nt + t, r))
    return pl.pallas_call(
        _hgrn_prompt_body,
        grid=(batch, nt),
        in_specs=[blk(0), blk(1), blk(2), blk(3),
                  pl.BlockSpec((None, 1, WIDTH), lambda b, t: (layer, 0, 0)),
                  pl.BlockSpec((None, 1, HEAD_DIM), lambda b, t: (layer, 0, 0))],
        out_specs=[pl.BlockSpec((H_TB, WIDTH), lambda b, t: (b * nt + t, 0)),
                   pl.BlockSpec((1, HEADS, HEAD_DIM, HEAD_DIM), lambda b, t: (b, 0, 0, 0))],
        out_shape=[jax.ShapeDtypeStruct((batch * seq, WIDTH), BF16),
                   jax.ShapeDtypeStruct((batch, HEADS, HEAD_DIM, HEAD_DIM), F32)],
        scratch_shapes=[pltpu.VMEM((HEADS, HEAD_DIM, HEAD_DIM), F32),
                        pltpu.VMEM((CHUNK, WIDTH), F32),
                        pltpu.VMEM((CHUNK, WIDTH), F32)],
        compiler_params=_params(("parallel", "arbitrary")),
        name="hgrn_prompt",
    )(proj, proj, proj, proj, lb, hg_norm)


def _gdn_prompt_body(xq_ref, xk_ref, xv_ref, wq_ref, wk_ref, wv_ref, z_ref, ab_ref, alog_ref,
                     dtb_ref, nw_ref, o_ref, s_ref, s_scr, q_scr, k_scr, v_scr, cq, ck, cv):
    t = pl.program_id(1)

    @pl.when(t == 0)
    def _():
        s_scr[...] = jnp.zeros_like(s_scr)
        for ext in (cq, ck, cv):
            ext[0:SUBLANES, :] = jnp.zeros((SUBLANES, WIDTH), F32)

    q_scr[...] = _l2norm_heads(_causal_conv_silu(xq_ref, wq_ref, cq), HEAD_DIM ** -0.5)
    k_scr[...] = _l2norm_heads(_causal_conv_silu(xk_ref, wk_ref, ck), 1.0)
    v_scr[...] = _causal_conv_silu(xv_ref, wv_ref, cv)

    nw = nw_ref[...]
    gb = _gdn_gate_tile(ab_ref[...], alog_ref[...], dtb_ref[...])
    nc = G_TB // CHUNK
    d_all = jnp.concatenate(
        [_cumsum_rows(gb[c * CHUNK:(c + 1) * CHUNK]) for c in range(nc)], axis=0)
    d_t = d_all.T
    for c in range(nc):
        rows = slice(c * CHUNK, (c + 1) * CHUNK)
        for h0 in range(0, HEADS, G_HG):
            heads = list(range(h0, h0 + G_HG))
            cols = [slice(h * HEAD_DIM, (h + 1) * HEAD_DIM) for h in heads]
            wy = _gdn_wy([q_scr[rows, cs] for cs in cols],
                         [k_scr[rows, cs] for cs in cols],
                         [v_scr[rows, cs] for cs in cols],
                         [d_all[rows, h:h + 1] for h in heads],
                         [d_t[h:h + 1, rows] for h in heads],
                         [gb[rows, HEADS + h:HEADS + h + 1] for h in heads])
            outs, new_states = _gdn_apply(wy, [s_scr[h] for h in heads])
            for i, h in enumerate(heads):
                s_scr[h] = new_states[i]
                o = _head_norm(outs[i], nw) * _silu(z_ref[rows, cols[i]])
                o_ref[rows, cols[i]] = o.astype(o_ref.dtype)

    @pl.when(t == pl.num_programs(1) - 1)
    def _():
        s_ref[0] = s_scr[...]


def _gdn_prompt(proj, conv_w, z, ab, alog, dtb, gdn_norm, layer, batch, seq):
    nt = seq // G_TB
    qkv_blk = 4 * WIDTH // WIDTH
    blk = lambda r: pl.BlockSpec((G_TB, WIDTH), lambda b, t: (b * nt + t, r))
    cw = lambda c: pl.BlockSpec((None, CONV_W, WIDTH), lambda b, t: (layer, 0, c))
    row = lambda n: pl.BlockSpec((None, 1, n), lambda b, t: (layer, 0, 0))
    return pl.pallas_call(
        _gdn_prompt_body,
        grid=(batch, nt),
        in_specs=[blk(qkv_blk), blk(qkv_blk + 1), blk(qkv_blk + 2), cw(0), cw(1), cw(2), blk(0),
                  pl.BlockSpec((G_TB, LANES), lambda b, t: (b * nt + t, 0)),
                  row(LANES), row(LANES), row(HEAD_DIM)],
        out_specs=[pl.BlockSpec((G_TB, WIDTH), lambda b, t: (b * nt + t, 0)),
                   pl.BlockSpec((1, HEADS, HEAD_DIM, HEAD_DIM), lambda b, t: (b, 0, 0, 0))],
        out_shape=[jax.ShapeDtypeStruct((batch * seq, WIDTH), BF16),
                   jax.ShapeDtypeStruct((batch, HEADS, HEAD_DIM, HEAD_DIM), F32)],
        scratch_shapes=[pltpu.VMEM((HEADS, HEAD_DIM, HEAD_DIM), F32)]
        + [pltpu.VMEM((G_TB, WIDTH), F32)] * 3 + [pltpu.VMEM((SUBLANES + G_TB, WIDTH), F32)] * 3,
        compiler_params=_params(("parallel", "arbitrary")),
        name="gdn_prompt",
    )(proj, proj, proj, conv_w, conv_w, conv_w, z, ab, alog, dtb, gdn_norm)


S_BB = 8
S_T = 4
S_C = 8
S_HB = 2


def _conv_sample_body(x_ref, buf_ref, w_ref, o_ref):
    c = pl.program_id(1)
    xx = [buf_ref[:, j, :] for j in range(CONV_W - 1)] + [x_ref[:, t, :] for t in range(S_T)]
    for t in range(S_T):
        y = xx[t] * w_ref[0:1, :]
        for j in range(1, CONV_W):
            y = y + xx[t + j] * w_ref[j:j + 1, :]
        y = _silu(y)
        scale = jnp.where(c == 0, HEAD_DIM ** -0.5, 1.0)
        yn = _l2norm_heads(y, scale)
        o_ref[:, t, :] = jnp.where(c == 2, y, yn)


def _conv_sample(proj3, buf, conv_w, layer):
    nb = proj3.shape[0]
    qkv_blk = 4
    return pl.pallas_call(
        _conv_sample_body,
        grid=(nb // S_BB, 3),
        in_specs=[pl.BlockSpec((S_BB, S_T, WIDTH), lambda i, c: (i, 0, qkv_blk + c)),
                  pl.BlockSpec((None, S_BB, CONV_W - 1, WIDTH), lambda i, c: (layer, i, 0, c)),
                  pl.BlockSpec((None, CONV_W, WIDTH), lambda i, c: (layer, 0, c))],
        out_specs=pl.BlockSpec((S_BB, S_T, WIDTH), lambda i, c: (i, 0, c)),
        out_shape=jax.ShapeDtypeStruct((nb, S_T, CONV_DIM), F32),
        compiler_params=_params(("parallel", "arbitrary")),
        name="conv_sample",
    )(proj3, buf, conv_w)


def _load_padded(dst_ref, src_ref):
    dst_ref[...] = jnp.zeros_like(dst_ref)
    dst_ref[0:S_T, :] = src_ref[0]


def _hgrn_sample_body(q_ref, f_ref, i_ref, gate_ref, lb_ref, nw_ref, s0_ref, _aliased_states,
                      o_ref, s_ref, qp, fp, ip, gp, op):
    nw = nw_ref[...]
    heads = range(HEADS)
    cols = [slice(h * HEAD_DIM, (h + 1) * HEAD_DIM) for h in heads]
    valid = lax.broadcasted_iota(jnp.int32, (S_C, WIDTH), 0) < S_T
    row = lax.broadcasted_iota(jnp.int32, (S_C, 1), 0)
    for bb in range(S_HB):
        for dst, src in ((qp, q_ref), (fp, f_ref), (ip, i_ref), (gp, gate_ref)):
            dst[bb] = jnp.zeros((S_C, WIDTH), F32)
            dst[bb, 0:S_T, :] = src[bb]
        q, k, logf = _hgrn_gates(qp[bb], fp[bb], lb_ref[...], valid)
        v = ip[bb]
        b = _cumsum_rows(logf)
        bl = b[S_T - 1:S_T, :]
        qe = q * jnp.exp(b)
        kdec = k * jnp.exp(bl - b)
        e_last = jnp.exp(bl)
        o_diag = [jnp.zeros((S_C, HEAD_DIM), F32) for _ in heads]
        for s in range(S_T):
            p = q * k[s:s + 1, :] * jnp.exp(jnp.minimum(b - b[s:s + 1, :], 0.0))
            col = [jnp.where(row >= s, jnp.sum(p[:, cols[h]], axis=-1, keepdims=True), 0.0)
                   for h in heads]
            o_diag = [o_diag[h] + col[h] * v[s:s + 1, cols[h]] for h in heads]
        s0 = [s0_ref[bb, h] for h in heads]
        o_inter = [_bdot(qe[:, cols[h]], s0[h]) for h in heads]
        upd = [_bdot_tn(kdec[:, cols[h]], v[:, cols[h]]) for h in heads]
        for h in heads:
            s_ref[bb, h] = _row_to_col(e_last[:, cols[h]]) * s0[h] + upd[h]
            op[bb, :, cols[h]] = _head_norm(o_inter[h] + o_diag[h], nw)
        o_ref[bb] = (op[bb] * _sigmoid(gp[bb]))[0:S_T, :]


def _hgrn_sample(proj3, lb, hg_norm, s0, new_states, layer):
    nb = proj3.shape[0]
    blk = lambda r: pl.BlockSpec((S_HB, S_T, WIDTH), lambda b: (b, 0, r))
    st = pl.BlockSpec((None, S_HB, HEADS, HEAD_DIM, HEAD_DIM), lambda b: (layer, b, 0, 0, 0))
    return pl.pallas_call(
        _hgrn_sample_body,
        grid=(nb // S_HB,),
        in_specs=[blk(0), blk(1), blk(2), blk(3),
                  pl.BlockSpec((None, 1, WIDTH), lambda b: (layer, 0, 0)),
                  pl.BlockSpec((None, 1, HEAD_DIM), lambda b: (layer, 0, 0)),
                  st, pl.BlockSpec(memory_space=pl.ANY)],
        out_specs=[pl.BlockSpec((S_HB, S_T, WIDTH), lambda b: (b, 0, 0)), st],
        out_shape=[jax.ShapeDtypeStruct((nb, S_T, WIDTH), F32),
                   jax.ShapeDtypeStruct(new_states.shape, F32)],
        input_output_aliases={7: 1},
        scratch_shapes=[pltpu.VMEM((S_HB, S_C, WIDTH), F32)] * 5,
        compiler_params=_params(("parallel",)),
        name="hgrn_sample",
    )(proj3, proj3, proj3, proj3, lb, hg_norm, s0, new_states)


def _gdn_sample_body(q_ref, k_ref, v_ref, z_ref, ab_ref, alog_ref, dtb_ref, nw_ref, s0_ref,
                     _aliased_states, o_ref, s_ref, qp, kp, vp, zp, abp, op):
    _load_padded(qp, q_ref)
    _load_padded(kp, k_ref)
    _load_padded(vp, v_ref)
    _load_padded(zp, z_ref)
    _load_padded(abp, ab_ref)
    nw = nw_ref[...]
    valid = lax.broadcasted_iota(jnp.int32, (S_C, LANES), 0) < S_T
    gb = jnp.where(valid, _gdn_gate_tile(abp[...], alog_ref[...], dtb_ref[...]), 0.0)
    d_all = _cumsum_rows(gb)
    row = lax.broadcasted_iota(jnp.int32, (S_C, 1), 0)
    heads = range(HEADS)
    cols = [slice(h * HEAD_DIM, (h + 1) * HEAD_DIM) for h in heads]
    qw, kw = qp[...], kp[...]
    qs = [qw[:, c] for c in cols]
    ks = [kw[:, c] for c in cols]
    ds = [d_all[:, h:h + 1] for h in heads]
    betas = [gb[:, HEADS + h:HEADS + h + 1] for h in heads]
    eds = [jnp.exp(d) for d in ds]
    Xs = [jnp.concatenate([vp[:, cols[h]] * betas[h], ks[h] * (betas[h] * eds[h])], axis=1)
          for h in heads]
    qkl = [[None] * S_T for _ in heads]
    for s in range(S_T):
        qk_w = qw * kw[s:s + 1, :]
        kk_w = kw * kw[s:s + 1, :]
        decay = [jnp.where(row >= s, jnp.exp(jnp.where(row >= s, ds[h] - ds[h][s:s + 1, :], 0.0)),
                           0.0) for h in heads]
        qk_s = [jnp.sum(qk_w[:, cols[h]], axis=-1, keepdims=True) for h in heads]
        for h in heads:
            qkl[h][s] = qk_s[h] * decay[h]
        if s < S_T - 1:
            kk_s = [jnp.sum(kk_w[:, cols[h]], axis=-1, keepdims=True) for h in heads]
            m_s = [jnp.where(row > s, betas[h] * kk_s[h] * decay[h], 0.0) for h in heads]
            Xs = [Xs[h] - m_s[h] * Xs[h][s:s + 1, :] for h in heads]
    d_last = [d[S_T - 1:S_T, :] for d in ds]
    qes = [qs[h] * eds[h] for h in heads]
    kdecs = [ks[h] * jnp.exp(d_last[h] - ds[h]) for h in heads]
    s0 = [s0_ref[0, h] for h in heads]
    sws = [_bdot(jnp.concatenate([Xs[h][:, HEAD_DIM:], qes[h]], axis=0), s0[h]) for h in heads]
    v_news = [Xs[h][:, :HEAD_DIM] - sws[h][:S_C] for h in heads]
    upds = [_bdot_tn(kdecs[h], v_news[h]) for h in heads]
    for h in heads:
        o = sws[h][S_C:]
        for s in range(S_T):
            o = o + qkl[h][s] * v_news[h][s:s + 1, :]
        s_ref[0, h] = jnp.exp(d_last[h]) * s0[h] + upds[h]
        op[:, cols[h]] = _head_norm(o, nw) * _silu(zp[:, cols[h]])
    o_ref[0] = op[0:S_T, :]


def _gdn_sample(qkv3, z3, ab3, alog, dtb, gdn_norm, s0, new_states, layer):
    nb = qkv3.shape[0]
    blk = lambda r: pl.BlockSpec((1, S_T, WIDTH), lambda b: (b, 0, r))
    st = pl.BlockSpec((None, 1, HEADS, HEAD_DIM, HEAD_DIM), lambda b: (layer, b, 0, 0, 0))
    return pl.pallas_call(
        _gdn_sample_body,
        grid=(nb,),
        in_specs=[blk(0), blk(1), blk(2), blk(0),
                  pl.BlockSpec((1, S_T, LANES), lambda b: (b, 0, 0)),
                  pl.BlockSpec((None, 1, LANES), lambda b: (layer, 0, 0)),
                  pl.BlockSpec((None, 1, LANES), lambda b: (layer, 0, 0)),
                  pl.BlockSpec((None, 1, HEAD_DIM), lambda b: (layer, 0, 0)),
                  st, pl.BlockSpec(memory_space=pl.ANY)],
        out_specs=[pl.BlockSpec((1, S_T, WIDTH), lambda b: (b, 0, 0)), st],
        out_shape=[jax.ShapeDtypeStruct((nb, S_T, WIDTH), F32),
                   jax.ShapeDtypeStruct(new_states.shape, F32)],
        input_output_aliases={9: 1},
        scratch_shapes=[pltpu.VMEM((S_C, WIDTH), F32)] * 4 + [pltpu.VMEM((S_C, LANES), F32),
                                                              pltpu.VMEM((S_C, WIDTH), F32)],
        compiler_params=_params(("parallel",)),
        name="gdn_sample",
    )(qkv3, qkv3, qkv3, z3, ab3, alog, dtb, gdn_norm, s0, new_states)


def _trunk(x, sample, states, W):
    rows = x.shape[0]
    tm = 512 if sample else 1024
    tm_ff = 512 if sample else 2048
    hg_states, gdn_states, conv_states = [], [], []
    if sample:
        hg_acc = jnp.zeros(states["hgrn"].shape, F32)
        gdn_acc = jnp.zeros(states["gdn"].shape, F32)
    for l in range(DEPTH):
        xn = _rmsnorm(x, W["ffn1_norm"][l], BF16)
        a = _swiglu_up(xn, W["ffn1_w_gu"], l, tm_ff)
        x = _down_residual(a, W["ffn1_w_down"], l, x, 0.5, 512, 512)

        xn = _rmsnorm(x, W["mix_norm"][l], BF16)
        proj = _matmul_nt(xn, W["w_in_t"], l, tm, 512, 0, MAIN_COLS)
        z = _matmul_nt(xn, W["w_z_t"], l, tm, 512, 0, WIDTH)
        ab = _matmul_nt(xn, W["w_in_t"], l, tm, LANES, MAIN_COLS // LANES, LANES)
        if sample:
            nb = rows // S_T
            proj3 = proj.reshape(nb, S_T, MAIN_COLS)
            ab3 = ab.reshape(nb, S_T, LANES)
            qkv3 = _conv_sample(proj3, states["conv"], W["conv_w"], l)
            o_hg, hg_acc = _hgrn_sample(proj3, W["lb"], W["hg_norm"], states["hgrn"], hg_acc, l)
            o_gdn, gdn_acc = _gdn_sample(qkv3, z.reshape(nb, S_T, WIDTH), ab3, W["alog"],
                                         W["dtb"], W["gdn_norm"], states["gdn"], gdn_acc, l)
            o_hg = o_hg.reshape(rows, WIDTH)
            o_gdn = o_gdn.reshape(rows, WIDTH)
            conv_new = proj3[:, S_T - (CONV_W - 1):, 4 * WIDTH:4 * WIDTH + CONV_DIM]
        else:
            batch, seq = states["batch"], states["seq"]
            o_hg, s_hg = _hgrn_prompt(proj, W["lb"], W["hg_norm"], l, batch, seq)
            o_gdn, s_gdn = _gdn_prompt(proj, W["conv_w"], z, ab, W["alog"], W["dtb"],
                                       W["gdn_norm"], l, batch, seq)
            conv_new = proj.reshape(batch, seq, MAIN_COLS)[
                :, seq - (CONV_W - 1):, 4 * WIDTH:4 * WIDTH + CONV_DIM]
        x = _mix_out_residual(o_hg, o_gdn, W["w_out"], l, x, tm, 512)

        xn = _rmsnorm(x, W["ffn2_norm"][l], BF16)
        a = _swiglu_up(xn, W["ffn2_w_gu"], l, tm_ff)
        x = _down_residual(a, W["ffn2_w_down"], l, x, 0.5, 512, 512)

        if not sample:
            hg_states.append(s_hg)
            gdn_states.append(s_gdn)
        conv_states.append(conv_new)
    y = _rmsnorm(x, W["final_norm"], F32)
    if not sample:
        hg_acc, gdn_acc = jnp.stack(hg_states), jnp.stack(gdn_states)
    return y, hg_acc, gdn_acc, jnp.stack(conv_states)


def kernel(x_prompt, x_sample, state_hgrn, state_gdn, state_conv, ffn1_norm, ffn1_w_gu, ffn1_w_down, mix_norm, w_in, hg_lower_bounds, hg_norm, gdn_conv_w, gdn_A_log, gdn_dt_bias, gdn_norm, w_out, ffn2_norm, ffn2_w_gu, ffn2_w_down, final_norm):
    batch, seq, _ = x_prompt.shape
    dec_batch, dec_seq, _ = x_sample.shape
    assert dec_seq == S_T and seq % H_TB == 0 and seq % G_TB == 0

    ab_lo = 4 * WIDTH + CONV_DIM
    w_in_t = jnp.swapaxes(w_in, 1, 2)
    row3 = lambda v: v.astype(F32)[:, None, :]
    pad_row = lambda v: row3(jnp.pad(v, ((0, 0), (0, LANES - v.shape[1]))))
    W = {
        "ffn1_norm": ffn1_norm, "mix_norm": mix_norm, "ffn2_norm": ffn2_norm,
        "final_norm": final_norm, "hg_norm": row3(hg_norm), "gdn_norm": row3(gdn_norm),
        "conv_w": gdn_conv_w,
        "ffn1_w_gu": ffn1_w_gu, "ffn1_w_down": ffn1_w_down.astype(BF16),
        "ffn2_w_gu": ffn2_w_gu, "ffn2_w_down": ffn2_w_down.astype(BF16),
        "w_out": w_out,
        "w_in_t": w_in_t,
        "w_z_t": w_in_t[:, ab_lo + 2 * HEADS:, :],
        "alog": pad_row(gdn_A_log), "dtb": pad_row(gdn_dt_bias),
        "lb": row3(_lower_bounds(hg_lower_bounds)),
    }

    y_p, hg_p, gdn_p, conv_p = _trunk(
        x_prompt.reshape(batch * seq, D_MODEL), False, {"batch": batch, "seq": seq}, W)
    y_s, hg_s, gdn_s, conv_s = _trunk(
        x_sample.reshape(dec_batch * dec_seq, D_MODEL), True,
        {"hgrn": state_hgrn, "gdn": state_gdn, "conv": state_conv}, W)
    return (y_p.reshape(batch, seq, D_MODEL), y_s.reshape(dec_batch, dec_seq, D_MODEL),
            hg_p, gdn_p, conv_p, hg_s, gdn_s, conv_s)
```
